```python
import math
import jax, jax.numpy as jnp
from jax import lax
import numpy as np

D_MODEL = 1024
BATCH = 16
SEQ = 2048
DEPTH = 1
DEC_BATCH = 8
DEC_SEQ = 8192
PAST_LEN = 128

A_W = 512
N_HEADS = 4
QK_NOPE = 128
QK_ROPE = 64
V_DIM = 128
Q_LORA = 384
KV_LORA = 256
MLA_W = N_HEADS * V_DIM
MIX_W = A_W + MLA_W
IN_COLS = 3 * A_W + Q_LORA + KV_LORA + QK_ROPE
D_FF = 2816
CONV_W = 3
QBLK = 128
ROPE_THETA = 10000.0
ATTN_SCALE = 1.0 / math.sqrt(QK_NOPE + QK_ROPE)
EPS = 1e-6

kernel_name = "hybrid_conv_mla_adaln_encoder"


def rmsnorm(x, g):
    xf = x.astype(jnp.float32)
    y = xf * lax.rsqrt(jnp.mean(xf * xf, axis=-1, keepdims=True) + EPS)
    return (y * g.astype(jnp.float32)).astype(x.dtype)


def dwconv3(x, w):
    xp = jnp.pad(x, ((0, 0), (1, 1), (0, 0)))
    return xp[:, :-2] * w[0] + xp[:, 1:-1] * w[1] + xp[:, 2:] * w[2]


def rope_tables(s):
    inv = 1.0 / (ROPE_THETA ** (jnp.arange(0, QK_ROPE, 2, dtype=jnp.float32) / QK_ROPE))
    ang = jnp.arange(s, dtype=jnp.float32)[:, None] * inv[None, :]
    return jnp.cos(ang), jnp.sin(ang)


def apply_rope(x, cos, sin):
    c = cos[None, :, None, :].astype(x.dtype)
    s = sin[None, :, None, :].astype(x.dtype)
    x1, x2 = jnp.split(x, 2, axis=-1)
    return jnp.concatenate([x1 * c - x2 * s, x2 * c + x1 * s], axis=-1)


def mla_attention(q_nope, q_rope, k_nope, k_rope, v):
    b, s, h, _ = q_nope.shape
    nb = s // QBLK
    qn_b = q_nope.reshape(b, nb, QBLK, h, QK_NOPE).transpose(1, 0, 2, 3, 4)
    qr_b = q_rope.reshape(b, nb, QBLK, h, QK_ROPE).transpose(1, 0, 2, 3, 4)

    def block(args):
        qn, qr = args
        sc = (jnp.einsum('bqhd,bkhd->bhqk', qn, k_nope)
              + jnp.einsum('bqhr,bkr->bhqk', qr, k_rope)).astype(jnp.float32) * ATTN_SCALE
        p = jax.nn.softmax(sc, axis=-1).astype(v.dtype)
        return jnp.einsum('bhqk,bkhd->bqhd', p, v)

    o = lax.map(block, (qn_b, qr_b))
    return o.transpose(1, 0, 2, 3, 4).reshape(b, s, h * V_DIM)


def forward(x, c, w_ada, b_ada, norm1_g, w_in, conv_a_w, q_norm_g, w_uq, kv_norm_g, w_ukv,
            out_norm_a_g, out_norm_b_g, w_o, norm2_g, w_up, ffn_conv_w, w_down, final_g):
    b, s, _ = x.shape
    cos, sin = rope_tables(s)
    c_act = jax.nn.silu(c)
    for l in range(DEPTH):
        mod = (c_act @ w_ada[l] + b_ada[l])[:, None, :]
        sh1, sc1, g1, sh2, sc2, g2 = jnp.split(mod, 6, axis=-1)

        h = rmsnorm(x, norm1_g[l]) * (1.0 + sc1) + sh1
        z = h @ w_in[l]
        cuts = np.cumsum([A_W, A_W, A_W, Q_LORA, KV_LORA]).tolist()
        h_a, b_a, c_a, c_q, c_kv, k_r = jnp.split(z, cuts, axis=-1)

        y_a = b_a * dwconv3(c_a * h_a, conv_a_w[l])

        q = (rmsnorm(c_q, q_norm_g[l]) @ w_uq[l]).reshape(b, s, N_HEADS, QK_NOPE + QK_ROPE)
        q_nope, q_rope = q[..., :QK_NOPE], apply_rope(q[..., QK_NOPE:], cos, sin)
        kv = (rmsnorm(c_kv, kv_norm_g[l]) @ w_ukv[l]).reshape(b, s, N_HEADS, QK_NOPE + V_DIM)
        k_nope, v = kv[..., :QK_NOPE], kv[..., QK_NOPE:]
        k_rope = apply_rope(k_r[:, :, None, :], cos, sin)[:, :, 0, :]
        y_b = mla_attention(q_nope, q_rope, k_nope, k_rope, v)

        y = jnp.concatenate([rmsnorm(y_a, out_norm_a_g[l]), rmsnorm(y_b, out_norm_b_g[l])], axis=-1) @ w_o[l]
        x = x + g1 * y

        h = rmsnorm(x, norm2_g[l]) * (1.0 + sc2) + sh2
        u = dwconv3(h @ w_up[l], ffn_conv_w[l])
        gate, val = jnp.split(u, 2, axis=-1)
        x = x + g2 * ((jax.nn.silu(gate) * val) @ w_down[l])
    return rmsnorm(x, final_g)


def setup_inputs(seed: int = 0) -> dict:
    key = jax.random.key(seed)
    ks = jax.random.split(key, 24)
    f32 = jnp.float32

    def nrm(k, shape, scale):
        return jax.random.normal(k, shape, f32) * scale

    def gain(k, shape):
        return 1.0 + 0.05 * jax.random.normal(k, shape, f32)

    L = DEPTH
    return {
        "x_prompt": nrm(ks[0], (BATCH, SEQ, D_MODEL), 1.0),
        "x_sample": nrm(ks[1], (DEC_BATCH, DEC_SEQ, D_MODEL), 1.0),
        "c_prompt": nrm(ks[2], (BATCH, D_MODEL), 1.0),
        "c_sample": nrm(ks[3], (DEC_BATCH, D_MODEL), 1.0),
        "w_ada": nrm(ks[4], (L, D_MODEL, 6 * D_MODEL), 0.3 * D_MODEL ** -0.5),
        "b_ada": nrm(ks[5], (L, 6 * D_MODEL), 0.02),
        "norm1_g": gain(ks[6], (L, D_MODEL)),
        "w_in": nrm(ks[7], (L, D_MODEL, IN_COLS), D_MODEL ** -0.5),
        "conv_a_w": nrm(ks[8], (L, CONV_W, A_W), 0.5),
        "q_norm_g": gain(ks[9], (L, Q_LORA)),
        "w_uq": nrm(ks[10], (L, Q_LORA, N_HEADS * (QK_NOPE + QK_ROPE)), Q_LORA ** -0.5),
        "kv_norm_g": gain(ks[11], (L, KV_LORA)),
        "w_ukv": nrm(ks[12], (L, KV_LORA, N_HEADS * (QK_NOPE + V_DIM)), KV_LORA ** -0.5),
        "out_norm_a_g": gain(ks[13], (L, A_W)),
        "out_norm_b_g": gain(ks[14], (L, MLA_W)),
        "w_o": nrm(ks[15], (L, MIX_W, D_MODEL), MIX_W ** -0.5),
        "norm2_g": gain(ks[16], (L, D_MODEL)),
        "w_up": nrm(ks[17], (L, D_MODEL, 2 * D_FF), D_MODEL ** -0.5),
        "ffn_conv_w": nrm(ks[18], (L, CONV_W, 2 * D_FF), 0.5),
        "w_down": nrm(ks[19], (L, D_FF, D_MODEL), D_FF ** -0.5),
        "final_g": gain(ks[20], (D_MODEL,)),
    }


def reference(x_prompt, x_sample, c_prompt, c_sample, w_ada, b_ada, norm1_g, w_in, conv_a_w,
              q_norm_g, w_uq, kv_norm_g, w_ukv, out_norm_a_g, out_norm_b_g, w_o, norm2_g,
              w_up, ffn_conv_w, w_down, final_g):
    y_prompt = forward(x_prompt, c_prompt, w_ada, b_ada, norm1_g, w_in, conv_a_w, q_norm_g, w_uq,
                       kv_norm_g, w_ukv, out_norm_a_g, out_norm_b_g, w_o, norm2_g, w_up,
                       ffn_conv_w, w_down, final_g)
    y_sample = forward(x_sample, c_sample, w_ada, b_ada, norm1_g, w_in, conv_a_w, q_norm_g, w_uq,
                       kv_norm_g, w_ukv, out_norm_a_g, out_norm_b_g, w_o, norm2_g, w_up,
                       ffn_conv_w, w_down, final_g)
    return (y_prompt, y_sample)
```

```python
import functools
import math

import jax
import jax.numpy as jnp
import numpy as np
from jax import lax
from jax.experimental import pallas as pl
from jax.experimental.pallas import tpu as pltpu

D_MODEL = 1024
A_W = 512
N_HEADS = 4
QK_NOPE = 128
QK_ROPE = 64
V_DIM = 128
Q_LORA = 384
KV_LORA = 256
MLA_W = N_HEADS * V_DIM
D_FF = 2816
ROPE_THETA = 10000.0
ATTN_SCALE = 1.0 / math.sqrt(QK_NOPE + QK_ROPE)
EPS = 1e-6

LANES = 128
HEAD_W = 2 * LANES
HALO = 16
FF_CHUNK = 256
N_FF_CHUNKS = D_FF // FF_CHUNK
VMEM_LIMIT = 56 * 1024 * 1024

BF16 = jnp.bfloat16
F32 = jnp.float32


def _rms(x, g):
    return x * lax.rsqrt(jnp.mean(x * x, axis=-1, keepdims=True) + EPS) * g


def _dot(a, b):
    return jnp.dot(a, b, preferred_element_type=F32)


def _const_spec(shape):
    nd = len(shape)
    return pl.BlockSpec(shape, lambda *_: (0,) * nd, pipeline_mode=pl.Buffered(1))


def _mod_kernel(c_ref, w_ref, b_ref, o_ref):
    c = c_ref[...]
    act = c / (1.0 + jnp.exp(-c))
    o_ref[...] = _dot(act.astype(BF16), w_ref[...]) + b_ref[...]


def _mod_call(c_all, w_ada, b_ada):
    nb = c_all.shape[0]
    n_out = w_ada.shape[1]
    bn = D_MODEL
    return pl.pallas_call(
        _mod_kernel,
        grid=(n_out // bn,),
        in_specs=[
            pl.BlockSpec((nb, D_MODEL), lambda j: (0, 0)),
            pl.BlockSpec((D_MODEL, bn), lambda j: (0, j)),
            pl.BlockSpec((1, bn), lambda j: (0, j)),
        ],
        out_specs=pl.BlockSpec((nb, bn), lambda j: (0, j)),
        out_shape=jax.ShapeDtypeStruct((nb, n_out), F32),
        name="adaln_mod",
    )(c_all, w_ada, b_ada)


def _pre_kernel(xp_ref, xm_ref, xn_ref, mod_ref, cs_ref, g1_ref, w1_ref, w2_ref, cw_ref,
                gq_ref, wq_ref, gkv_ref, wkv_ref, ga_ref,
                ya_ref, q_ref, k_ref, v_ref,
                h_buf, p_buf, *, ts):
    i = pl.program_id(1)
    n_i = pl.num_programs(1)
    g1 = g1_ref[...]
    scale = 1.0 + mod_ref[0, 1:2, :]
    shift = mod_ref[0, 0:1, :]

    def norm_mod(x):
        return (_rms(x, g1) * scale + shift).astype(BF16)

    h_buf[0:HALO, :] = norm_mod(xp_ref[0])
    h_buf[HALO:HALO + ts, :] = norm_mod(xm_ref[0])
    h_buf[HALO + ts:, :] = norm_mod(xn_ref[0])

    z1 = _dot(h_buf[...], w1_ref[...])
    p = z1[:, :A_W] * z1[:, A_W:]
    p_buf[...] = p
    p_buf[0:HALO, :] = jnp.where(i > 0, p[0:HALO], 0.0)
    p_buf[HALO + ts:, :] = jnp.where(i < n_i - 1, p[HALO + ts:], 0.0)
    cw = cw_ref[...]
    conv = (p_buf[HALO - 1:HALO - 1 + ts, :] * cw[0:1]
            + p_buf[HALO:HALO + ts, :] * cw[1:2]
            + p_buf[HALO + 1:HALO + 1 + ts, :] * cw[2:3])

    z2 = _dot(h_buf[HALO:HALO + ts, :], w2_ref[...])
    y_a = z2[:, :A_W] * conv
    ya_ref[0] = _rms(y_a, ga_ref[...]).astype(BF16)

    cs = cs_ref[...]
    keep = lax.broadcasted_iota(jnp.int32, (1, LANES), 1) < QK_ROPE

    def rope(pair):
        t = pair * cs
        return jnp.where(keep, t + pltpu.roll(t, QK_ROPE, axis=1), 0.0)

    o = A_W
    c_q = z2[:, o:o + Q_LORA]
    o += Q_LORA
    c_kv = z2[:, o:o + KV_LORA]
    o += KV_LORA
    k_rope = rope(z2[:, o:o + LANES]).astype(BF16)

    q = _dot(_rms(c_q, gq_ref[...]).astype(BF16), wq_ref[...])
    kv = _dot(_rms(c_kv, gkv_ref[...]).astype(BF16), wkv_ref[...])
    for h in range(N_HEADS):
        b0 = h * HEAD_W
        q_ref[0, :, b0:b0 + LANES] = q[:, b0:b0 + LANES].astype(BF16)
        q_ref[0, :, b0 + LANES:b0 + HEAD_W] = rope(q[:, b0 + LANES:b0 + HEAD_W]).astype(BF16)
        k_ref[0, :, b0:b0 + LANES] = kv[:, h * QK_NOPE:(h + 1) * QK_NOPE].astype(BF16)
        k_ref[0, :, b0 + LANES:b0 + HEAD_W] = k_rope
    v_ref[0] = kv[:, N_HEADS * QK_NOPE:].astype(BF16)


def _pre_call(x, mod, cs, wts, *, ts):
    b, s, _ = x.shape
    n_i = s // ts
    hb = ts // HALO
    n_hb = s // HALO
    kern = functools.partial(_pre_kernel, ts=ts)
    tile = lambda c, dt: jax.ShapeDtypeStruct((b, s, c), dt)
    out_spec = lambda c: pl.BlockSpec((1, ts, c), lambda bi, i: (bi, i, 0))
    w_list = [wts["g1"], wts["w1"], wts["w2"], wts["conv_a"], wts["gq"], wts["wq"], wts["gkv"], wts["wkv"],
              wts["ga"]]
    return pl.pallas_call(
        kern,
        grid=(b, n_i),
        in_specs=[
            pl.BlockSpec((1, HALO, D_MODEL), lambda bi, i: (bi, jnp.maximum(i * hb - 1, 0), 0)),
            pl.BlockSpec((1, ts, D_MODEL), lambda bi, i: (bi, i, 0)),
            pl.BlockSpec((1, HALO, D_MODEL), lambda bi, i: (bi, jnp.minimum((i + 1) * hb, n_hb - 1), 0)),
            pl.BlockSpec((1, 6, D_MODEL), lambda bi, i: (bi, 0, 0)),
            pl.BlockSpec((ts, LANES), lambda bi, i: (i, 0)),
        ] + [_const_spec(w.shape) for w in w_list],
        out_specs=[out_spec(A_W), out_spec(N_HEADS * HEAD_W), out_spec(N_HEADS * HEAD_W), out_spec(MLA_W)],
        out_shape=[tile(A_W, BF16), tile(N_HEADS * HEAD_W, BF16), tile(N_HEADS * HEAD_W, BF16),
                   tile(MLA_W, BF16)],
        scratch_shapes=[pltpu.VMEM((ts + 2 * HALO, D_MODEL), BF16),
                        pltpu.VMEM((ts + 2 * HALO, A_W), F32)],
        compiler_params=pltpu.CompilerParams(dimension_semantics=("parallel", "arbitrary"),
                                             vmem_limit_bytes=VMEM_LIMIT),
        name="pre_attn",
    )(x, x, x, mod, cs, *w_list)


def _attn_kernel(q_ref, k_ref, v_ref, o_ref, *, tk):
    s = k_ref.shape[1]
    tq = q_ref.shape[1]
    q = q_ref[0]
    c = ATTN_SCALE * math.log2(math.e)

    def step(j, carry):
        m, l, acc = carry
        k0 = pl.multiple_of(j * tk, tk)
        kc = k_ref[0, pl.ds(k0, tk), :]
        vc = v_ref[0, pl.ds(k0, tk), :]
        sc = lax.dot_general(q, kc, (((1,), (1,)), ((), ())), preferred_element_type=F32)
        m_new = jnp.maximum(m, jnp.max(sc, axis=1, keepdims=True))
        alpha = jnp.exp2((m - m_new) * c)
        p = jnp.exp2((sc - m_new) * c)
        l = l * alpha + jnp.sum(p, axis=1, keepdims=True)
        acc = acc * alpha + _dot(p.astype(BF16), vc)
        return m_new, l, acc

    init = (jnp.full((tq, 1), -jnp.inf, F32), jnp.zeros((tq, 1), F32), jnp.zeros((tq, V_DIM), F32))
    _, l, acc = lax.fori_loop(0, s // tk, step, init)
    o_ref[0] = acc / l


def _attn_call(q, k, v, *, tq, tk):
    b, s, _ = q.shape
    kern = functools.partial(_attn_kernel, tk=tk)
    return pl.pallas_call(
        kern,
        grid=(b, N_HEADS, s // tq),
        in_specs=[
            pl.BlockSpec((1, tq, HEAD_W), lambda bi, h, i: (bi, i, h)),
            pl.BlockSpec((1, s, HEAD_W), lambda bi, h, i: (bi, 0, h)),
            pl.BlockSpec((1, s, V_DIM), lambda bi, h, i: (bi, 0, h)),
        ],
        out_specs=pl.BlockSpec((1, tq, V_DIM), lambda bi, h, i: (bi, i, h)),
        out_shape=jax.ShapeDtypeStruct((b, s, MLA_W), F32),
        compiler_params=pltpu.CompilerParams(dimension_semantics=("parallel", "parallel", "arbitrary"),
                                             vmem_limit_bytes=VMEM_LIMIT),
        name="mla_attn",
    )(q, k, v)


def _post_kernel(xp_ref, xm_ref, xn_ref, yap_ref, yam_ref, yan_ref, ybp_ref, ybm_ref, ybn_ref, mod_ref,
                 gb_ref, wo_ref, g2_ref, wup_ref, cw_ref, wdn_ref, gf_ref,
                 o_ref,
                 h_buf, a_buf, acc_buf, *, ts):
    i = pl.program_id(1)
    n_i = pl.num_programs(1)
    gb = gb_ref[...]
    g2 = g2_ref[...]
    wo = wo_ref[...]
    gate1 = mod_ref[0, 2:3, :]
    shift2 = mod_ref[0, 3:4, :]
    scale2 = 1.0 + mod_ref[0, 4:5, :]
    gate2 = mod_ref[0, 5:6, :]

    def mix_residual(x, ya, yb):
        cat = jnp.concatenate([ya, _rms(yb, gb).astype(BF16)], axis=-1)
        return x + gate1 * _dot(cat, wo)

    def norm_mod(x1):
        return _rms(x1, g2) * scale2 + shift2

    hp = norm_mod(mix_residual(xp_ref[0], yap_ref[0], ybp_ref[0]))
    h_buf[0:HALO, :] = jnp.where(i > 0, hp, 0.0).astype(BF16)
    x1 = mix_residual(xm_ref[0], yam_ref[0], ybm_ref[0])
    h_buf[HALO:HALO + ts, :] = norm_mod(x1).astype(BF16)
    hn = norm_mod(mix_residual(xn_ref[0], yan_ref[0], ybn_ref[0]))
    h_buf[HALO + ts:, :] = jnp.where(i < n_i - 1, hn, 0.0).astype(BF16)

    acc_buf[...] = jnp.zeros_like(acc_buf)

    def chunk(c, carry):
        a_buf[...] = _dot(h_buf[...], wup_ref[c])
        cw = cw_ref[c]
        u = (a_buf[HALO - 1:HALO - 1 + ts, :] * cw[0:1]
             + a_buf[HALO:HALO + ts, :] * cw[1:2]
             + a_buf[HALO + 1:HALO + 1 + ts, :] * cw[2:3])
        g = u[:, :FF_CHUNK]
        act = g / (1.0 + jnp.exp(-g)) * u[:, FF_CHUNK:]
        acc_buf[...] += _dot(act.astype(BF16), wdn_ref[c])
        return carry

    lax.fori_loop(0, N_FF_CHUNKS, chunk, 0)
    x2 = x1 + gate2 * acc_buf[...]
    o_ref[0] = _rms(x2, gf_ref[...])


def _post_call(x, ya, yb, mod, wts, *, ts):
    b, s, _ = x.shape
    n_i = s // ts
    hb = ts // HALO
    n_hb = s // HALO
    kern = functools.partial(_post_kernel, ts=ts)

    def three(c):
        return [
            pl.BlockSpec((1, HALO, c), lambda bi, i: (bi, jnp.maximum(i * hb - 1, 0), 0)),
            pl.BlockSpec((1, ts, c), lambda bi, i: (bi, i, 0)),
            pl.BlockSpec((1, HALO, c), lambda bi, i: (bi, jnp.minimum((i + 1) * hb, n_hb - 1), 0)),
        ]

    w_list = [wts["gb"], wts["wo"], wts["g2"], wts["wup"], wts["conv_f"], wts["wdn"], wts["gf"]]
    return pl.pallas_call(
        kern,
        grid=(b, n_i),
        in_specs=three(D_MODEL) + three(A_W) + three(MLA_W)
        + [pl.BlockSpec((1, 6, D_MODEL), lambda bi, i: (bi, 0, 0))]
        + [_const_spec(w.shape) for w in w_list],
        out_specs=pl.BlockSpec((1, ts, D_MODEL), lambda bi, i: (bi, i, 0)),
        out_shape=jax.ShapeDtypeStruct((b, s, D_MODEL), F32),
        scratch_shapes=[pltpu.VMEM((ts + 2 * HALO, D_MODEL), BF16),
                        pltpu.VMEM((ts + 2 * HALO, 2 * FF_CHUNK), F32),
                        pltpu.VMEM((ts, D_MODEL), F32)],
        compiler_params=pltpu.CompilerParams(dimension_semantics=("parallel", "arbitrary"),
                                             vmem_limit_bytes=VMEM_LIMIT),
        name="post_mlp",
    )(x, x, x, ya, ya, ya, yb, yb, yb, mod, *w_list)


def _rot_cols(w):
    half = w.shape[1] // 2
    return jnp.concatenate([-w[:, half:], w[:, :half]], axis=1)


def _prep_weights(norm1_g, w_in, conv_a_w, q_norm_g, w_uq, kv_norm_g, w_ukv, out_norm_a_g, out_norm_b_g,
                  w_o, norm2_g, w_up, ffn_conv_w, w_down, final_g):
    cuts = np.cumsum([A_W, A_W, A_W, Q_LORA, KV_LORA]).tolist()
    w_h, w_b, w_c, w_cq, w_ckv, w_kr = jnp.split(w_in, cuts, axis=1)
    w1 = jnp.concatenate([w_h, w_c], axis=1)
    w2 = jnp.concatenate([w_b, w_cq, w_ckv, w_kr, _rot_cols(w_kr)], axis=1)

    wq3 = w_uq.reshape(Q_LORA, N_HEADS, QK_NOPE + QK_ROPE)
    wq_rope = wq3[:, :, QK_NOPE:]
    wq_rot = jnp.concatenate([-wq_rope[:, :, QK_ROPE // 2:], wq_rope[:, :, :QK_ROPE // 2]], axis=2)
    wq = jnp.concatenate([wq3, wq_rot], axis=2).reshape(Q_LORA, N_HEADS * HEAD_W)

    wkv3 = w_ukv.reshape(KV_LORA, N_HEADS, QK_NOPE + V_DIM)
    wkv = jnp.concatenate([wkv3[:, :, :QK_NOPE].reshape(KV_LORA, -1), wkv3[:, :, QK_NOPE:].reshape(KV_LORA, -1)],
                          axis=1)

    wup = w_up.reshape(D_MODEL, 2, N_FF_CHUNKS, FF_CHUNK).transpose(2, 0, 1, 3).reshape(
        N_FF_CHUNKS, D_MODEL, 2 * FF_CHUNK)
    conv_f = ffn_conv_w.reshape(3, 2, N_FF_CHUNKS, FF_CHUNK).transpose(2, 0, 1, 3).reshape(
        N_FF_CHUNKS, 3, 2 * FF_CHUNK)
    wdn = w_down.reshape(N_FF_CHUNKS, FF_CHUNK, D_MODEL)
    row = lambda g: g.reshape(1, -1)
    return {
        "g1": row(norm1_g), "w1": w1.astype(BF16), "w2": w2.astype(BF16), "conv_a": conv_a_w,
        "gq": row(q_norm_g), "wq": wq.astype(BF16), "gkv": row(kv_norm_g), "wkv": wkv.astype(BF16),
        "ga": row(out_norm_a_g), "gb": row(out_norm_b_g), "wo": w_o.astype(BF16), "g2": row(norm2_g),
        "wup": wup.astype(BF16), "conv_f": conv_f, "wdn": wdn.astype(BF16), "gf": row(final_g),
    }


def _rope_table(s):
    inv = 1.0 / (ROPE_THETA ** (jnp.arange(0, QK_ROPE, 2, dtype=F32) / QK_ROPE))
    ang = jnp.arange(s, dtype=F32)[:, None] * inv[None, :]
    cos, sin = jnp.cos(ang), jnp.sin(ang)
    return jnp.concatenate([cos, cos, sin, sin], axis=1)


def _tiles(s):
    ts = min(512, s)
    return ts, min(512, s), min(512, s)


def _forward(x, mod, wts):
    s = x.shape[1]
    ts, tq, tk = _tiles(s)
    ya, q, k, v = _pre_call(x, mod, _rope_table(s), wts, ts=ts)
    yb = _attn_call(q, k, v, tq=tq, tk=tk)
    return _post_call(x, ya, yb, mod, wts, ts=ts)


def kernel(x_prompt, x_sample, c_prompt, c_sample, w_ada, b_ada, norm1_g, w_in, conv_a_w, q_norm_g, w_uq,
           kv_norm_g, w_ukv, out_norm_a_g, out_norm_b_g, w_o, norm2_g, w_up, ffn_conv_w, w_down, final_g):
    assert w_ada.shape[0] == 1, "single-layer kernel"
    wts = _prep_weights(norm1_g[0], w_in[0], conv_a_w[0], q_norm_g[0], w_uq[0], kv_norm_g[0], w_ukv[0],
                        out_norm_a_g[0], out_norm_b_g[0], w_o[0], norm2_g[0], w_up[0], ffn_conv_w[0],
                        w_down[0], final_g)
    nbp = c_prompt.shape[0]
    c_all = jnp.concatenate([c_prompt, c_sample], axis=0)
    mod = _mod_call(c_all, w_ada[0].astype(BF16), b_ada[0].reshape(1, -1))
    mod = mod.reshape(c_all.shape[0], 6, D_MODEL)
    y_prompt = _forward(x_prompt, mod[:nbp], wts)
    y_sample = _forward(x_sample, mod[nbp:], wts)
    return (y_prompt, y_sample)
```

```python
import functools
import math

import jax
import jax.numpy as jnp
import numpy as np
from jax import lax
from jax.experimental import pallas as pl
from jax.experimental.pallas import tpu as pltpu

D_MODEL = 1024
A_W = 512
N_HEADS = 4
QK_NOPE = 128
QK_ROPE = 64
V_DIM = 128
Q_LORA = 384
KV_LORA = 256
MLA_W = N_HEADS * V_DIM
D_FF = 2816
ROPE_THETA = 10000.0
ATTN_SCALE = 1.0 / math.sqrt(QK_NOPE + QK_ROPE)
Q_PRESCALE = ATTN_SCALE * math.log2(math.e)
EPS = 1e-6

LANES = 128
HEAD_W = 2 * LANES
HALO = 16
SUM_ROWS = 16
FF_CHUNK = 256
N_FF_CHUNKS = D_FF // FF_CHUNK
VMEM_LIMIT = 56 * 1024 * 1024

BF16 = jnp.bfloat16
F32 = jnp.float32


def _rms(x, g):
    return x * lax.rsqrt(jnp.mean(x * x, axis=-1, keepdims=True) + EPS) * g


def _dot(a, b):
    return jnp.dot(a, b, preferred_element_type=F32)


def _dot_nt(a, b):
    return lax.dot_general(a, b, (((1,), (1,)), ((), ())), preferred_element_type=F32)


def _const_spec(shape):
    nd = len(shape)
    return pl.BlockSpec(shape, lambda *_: (0,) * nd, pipeline_mode=pl.Buffered(1))


def _mod_kernel(c_ref, w_ref, b_ref, o_ref):
    c = c_ref[...]
    act = c / (1.0 + jnp.exp(-c))
    o_ref[...] = _dot(act.astype(BF16), w_ref[...]) + b_ref[...]


def _mod_call(c_all, w_ada, b_ada):
    nb = c_all.shape[0]
    n_out = w_ada.shape[1]
    bn = D_MODEL
    return pl.pallas_call(
        _mod_kernel,
        grid=(n_out // bn,),
        in_specs=[
            pl.BlockSpec((nb, D_MODEL), lambda j: (0, 0)),
            pl.BlockSpec((D_MODEL, bn), lambda j: (0, j)),
            pl.BlockSpec((1, bn), lambda j: (0, j)),
        ],
        out_specs=pl.BlockSpec((nb, bn), lambda j: (0, j)),
        out_shape=jax.ShapeDtypeStruct((nb, n_out), F32),
        name="adaln_mod",
    )(c_all, w_ada, b_ada)


def _pre_kernel(xp_ref, xm_ref, xn_ref, mod_ref, cs_ref, cst_ref, g1_ref, w1_ref, w2_ref, cw_ref,
                gq_ref, wqt_ref, gkv_ref, wk_ref, wvt_ref, ga_ref,
                ya_ref, qt_ref, k_ref, vt_ref,
                h_buf, p_buf, *, ts):
    i = pl.program_id(1)
    n_i = pl.num_programs(1)
    g1 = g1_ref[...]
    scale = 1.0 + mod_ref[0, 1:2, :]
    shift = mod_ref[0, 0:1, :]

    def norm_mod(x):
        return (_rms(x, g1) * scale + shift).astype(BF16)

    h_buf[0:HALO, :] = norm_mod(xp_ref[0])
    h_buf[HALO:HALO + ts, :] = norm_mod(xm_ref[0])
    h_buf[HALO + ts:, :] = norm_mod(xn_ref[0])

    z1 = _dot(h_buf[...], w1_ref[...])
    p = z1[:, :A_W] * z1[:, A_W:]
    p_buf[...] = p
    p_buf[0:HALO, :] = jnp.where(i > 0, p[0:HALO], 0.0)
    p_buf[HALO + ts:, :] = jnp.where(i < n_i - 1, p[HALO + ts:], 0.0)
    cw = cw_ref[...]
    conv = (p_buf[HALO - 1:HALO - 1 + ts, :] * cw[0:1]
            + p_buf[HALO:HALO + ts, :] * cw[1:2]
            + p_buf[HALO + 1:HALO + 1 + ts, :] * cw[2:3])

    z2 = _dot(h_buf[HALO:HALO + ts, :], w2_ref[...])
    y_a = z2[:, :A_W] * conv
    ya_ref[0] = _rms(y_a, ga_ref[...]).astype(BF16)

    cs = cs_ref[...]
    keep = lax.broadcasted_iota(jnp.int32, (1, LANES), 1) < QK_ROPE
    o = A_W
    c_q = z2[:, o:o + Q_LORA]
    o += Q_LORA
    c_kv = z2[:, o:o + KV_LORA]
    o += KV_LORA
    t = z2[:, o:o + LANES] * cs
    k_rope = jnp.where(keep, t + pltpu.roll(t, QK_ROPE, axis=1), 0.0).astype(BF16)

    cqn = _rms(c_q, gq_ref[...]).astype(BF16)
    ckvn = _rms(c_kv, gkv_ref[...]).astype(BF16)
    qt = _dot_nt(wqt_ref[...], cqn)
    k_nope = _dot(ckvn, wk_ref[...])
    vt = _dot_nt(wvt_ref[...], ckvn)
    cos_t = cst_ref[0:QK_ROPE, :]
    sin_t = cst_ref[QK_ROPE:, :]
    for h in range(N_HEADS):
        b0 = h * HEAD_W
        r0 = b0 + QK_NOPE
        qt_ref[0, h, 0:QK_NOPE, :] = (qt[b0:r0] * Q_PRESCALE).astype(BF16)
        q_rope = qt[r0:r0 + QK_ROPE] * cos_t + qt[r0 + QK_ROPE:b0 + HEAD_W] * sin_t
        qt_ref[0, h, QK_NOPE:QK_NOPE + QK_ROPE, :] = (q_rope * Q_PRESCALE).astype(BF16)
        qt_ref[0, h, QK_NOPE + QK_ROPE:, :] = jnp.zeros((HEAD_W - QK_NOPE - QK_ROPE, ts), BF16)
        k_ref[0, h, :, 0:LANES] = k_nope[:, h * QK_NOPE:(h + 1) * QK_NOPE].astype(BF16)
        k_ref[0, h, :, LANES:] = k_rope
        vt_ref[0, h, 0] = vt[h * V_DIM:(h + 1) * V_DIM].astype(BF16)


def _pre_call(x, mod, cs, wts, *, ts):
    b, s, _ = x.shape
    n_i = s // ts
    hb = ts // HALO
    n_hb = s // HALO
    kern = functools.partial(_pre_kernel, ts=ts)
    w_list = [wts["g1"], wts["w1"], wts["w2"], wts["conv_a"], wts["gq"], wts["wqt"], wts["gkv"], wts["wk"],
              wts["wvt"], wts["ga"]]
    return pl.pallas_call(
        kern,
        grid=(b, n_i),
        in_specs=[
            pl.BlockSpec((1, HALO, D_MODEL), lambda bi, i: (bi, jnp.maximum(i * hb - 1, 0), 0)),
            pl.BlockSpec((1, ts, D_MODEL), lambda bi, i: (bi, i, 0)),
            pl.BlockSpec((1, HALO, D_MODEL), lambda bi, i: (bi, jnp.minimum((i + 1) * hb, n_hb - 1), 0)),
            pl.BlockSpec((1, 6, D_MODEL), lambda bi, i: (bi, 0, 0)),
            pl.BlockSpec((ts, LANES), lambda bi, i: (i, 0)),
            pl.BlockSpec((LANES, ts), lambda bi, i: (0, i)),
        ] + [_const_spec(w.shape) for w in w_list],
        out_specs=[
            pl.BlockSpec((1, ts, A_W), lambda bi, i: (bi, i, 0)),
            pl.BlockSpec((1, N_HEADS, HEAD_W, ts), lambda bi, i: (bi, 0, 0, i)),
            pl.BlockSpec((1, N_HEADS, ts, HEAD_W), lambda bi, i: (bi, 0, i, 0)),
            pl.BlockSpec((1, N_HEADS, 1, V_DIM, ts), lambda bi, i: (bi, 0, i, 0, 0)),
        ],
        out_shape=[
            jax.ShapeDtypeStruct((b, s, A_W), BF16),
            jax.ShapeDtypeStruct((b, N_HEADS, HEAD_W, s), BF16),
            jax.ShapeDtypeStruct((b, N_HEADS, s, HEAD_W), BF16),
            jax.ShapeDtypeStruct((b, N_HEADS, n_i, V_DIM, ts), BF16),
        ],
        scratch_shapes=[pltpu.VMEM((ts + 2 * HALO, D_MODEL), BF16),
                        pltpu.VMEM((ts + 2 * HALO, A_W), F32)],
        compiler_params=pltpu.CompilerParams(dimension_semantics=("parallel", "arbitrary"),
                                             vmem_limit_bytes=VMEM_LIMIT),
        name="pre_attn",
    )(x, x, x, mod, cs, cs.T, *w_list)


def _attn_kernel(qt_ref, k_ref, vt_ref, gb_ref, o_ref, s_buf, acc_buf):
    n_chunks, _, tk = vt_ref.shape[2:]
    tq = qt_ref.shape[3]
    n_items = N_HEADS * n_chunks
    ones = jnp.ones((SUM_ROWS, tk), BF16)

    def scores(n, slot):
        h = lax.div(n, n_chunks)
        k0 = pl.multiple_of(lax.rem(n, n_chunks) * tk, tk)
        st = _dot(k_ref[0, h, pl.ds(k0, tk), :], qt_ref[0, h])
        s_buf[slot] = st
        return jnp.max(st, axis=0, keepdims=True)

    def accumulate(n, slot, cmax, m, acc):
        h = lax.div(n, n_chunks)
        j = lax.rem(n, n_chunks)
        m = jnp.where(j == 0, -jnp.inf, m)
        m_new = jnp.maximum(m, cmax)
        alpha = jnp.exp2(m - m_new)
        pt = jnp.exp2(s_buf[slot] - m_new).astype(BF16)
        v_ext = jnp.concatenate([vt_ref[0, h, j], ones], axis=0)
        acc = acc * alpha + _dot(v_ext, pt)
        acc_buf[h] = acc
        return m_new, acc

    def pair(nn, carry):
        m, acc, cmax0 = carry
        n = 2 * nn
        cmax1 = scores(n + 1, 1)
        m, acc = accumulate(n, 0, cmax0, m, acc)
        cmax0 = scores(jnp.minimum(n + 2, n_items - 1), 0)
        m, acc = accumulate(n + 1, 1, cmax1, m, acc)
        return m, acc, cmax0

    init = (jnp.full((1, tq), -jnp.inf, F32), jnp.zeros((V_DIM + SUM_ROWS, tq), F32), scores(0, 0))
    lax.fori_loop(0, n_items // 2, pair, init)
    heads = []
    for h in range(N_HEADS):
        acc = acc_buf[h]
        heads.append((acc[:V_DIM] / acc[V_DIM:V_DIM + 1]).T)
    y_b = jnp.concatenate(heads, axis=1)
    o_ref[0] = _rms(y_b, gb_ref[...]).astype(BF16)


def _attn_call(qt, k, vt, gb, *, tq):
    b, _, _, s = qt.shape
    n_chunks, _, tk = vt.shape[2:]
    resident = pl.Buffered(1)
    return pl.pallas_call(
        _attn_kernel,
        grid=(b, s // tq),
        in_specs=[
            pl.BlockSpec((1, N_HEADS, HEAD_W, tq), lambda bi, i: (bi, 0, 0, i)),
            pl.BlockSpec((1, N_HEADS, s, HEAD_W), lambda bi, i: (bi, 0, 0, 0), pipeline_mode=resident),
            pl.BlockSpec((1, N_HEADS, n_chunks, V_DIM, tk), lambda bi, i: (bi, 0, 0, 0, 0),
                         pipeline_mode=resident),
            _const_spec(gb.shape),
        ],
        out_specs=pl.BlockSpec((1, tq, MLA_W), lambda bi, i: (bi, i, 0)),
        out_shape=jax.ShapeDtypeStruct((b, s, MLA_W), BF16),
        scratch_shapes=[pltpu.VMEM((2, tk, tq), F32),
                        pltpu.VMEM((N_HEADS, V_DIM + SUM_ROWS, tq), F32)],
        compiler_params=pltpu.CompilerParams(dimension_semantics=("parallel", "arbitrary"),
                                             vmem_limit_bytes=VMEM_LIMIT),
        name="mla_attn",
    )(qt, k, vt, gb)


def _post_kernel(xp_ref, xm_ref, xn_ref, yap_ref, yam_ref, yan_ref, ybp_ref, ybm_ref, ybn_ref, mod_ref,
                 wo_ref, g2_ref, wup_ref, cw_ref, wdn_ref, gf_ref,
                 o_ref,
                 h_buf, a_buf, acc_buf, *, ts):
    i = pl.program_id(1)
    n_i = pl.num_programs(1)
    g2 = g2_ref[...]
    wo = wo_ref[...]
    gate1 = mod_ref[0, 2:3, :]
    shift2 = mod_ref[0, 3:4, :]
    scale2 = 1.0 + mod_ref[0, 4:5, :]
    gate2 = mod_ref[0, 5:6, :]

    def mix_residual(x, ya, yb):
        cat = jnp.concatenate([ya, yb], axis=-1)
        return x + gate1 * _dot(cat, wo)

    def norm_mod(x1):
        return _rms(x1, g2) * scale2 + shift2

    hp = norm_mod(mix_residual(xp_ref[0], yap_ref[0], ybp_ref[0]))
    h_buf[0:HALO, :] = jnp.where(i > 0, hp, 0.0).astype(BF16)
    x1 = mix_residual(xm_ref[0], yam_ref[0], ybm_ref[0])
    h_buf[HALO:HALO + ts, :] = norm_mod(x1).astype(BF16)
    hn = norm_mod(mix_residual(xn_ref[0], yan_ref[0], ybn_ref[0]))
    h_buf[HALO + ts:, :] = jnp.where(i < n_i - 1, hn, 0.0).astype(BF16)

    acc_buf[...] = jnp.zeros_like(acc_buf)

    def chunk(c, carry):
        a_buf[...] = _dot(h_buf[...], wup_ref[c])
        cw = cw_ref[c]
        u = (a_buf[HALO - 1:HALO - 1 + ts, :] * cw[0:1]
             + a_buf[HALO:HALO + ts, :] * cw[1:2]
             + a_buf[HALO + 1:HALO + 1 + ts, :] * cw[2:3])
        g = u[:, :FF_CHUNK]
        act = g / (1.0 + jnp.exp(-g)) * u[:, FF_CHUNK:]
        acc_buf[...] += _dot(act.astype(BF16), wdn_ref[c])
        return carry

    lax.fori_loop(0, N_FF_CHUNKS, chunk, 0)
    x2 = x1 + gate2 * acc_buf[...]
    o_ref[0] = _rms(x2, gf_ref[...])


def _post_call(x, ya, yb, mod, wts, *, ts):
    b, s, _ = x.shape
    n_i = s // ts
    hb = ts // HALO
    n_hb = s // HALO
    kern = functools.partial(_post_kernel, ts=ts)

    def three(c):
        return [
            pl.BlockSpec((1, HALO, c), lambda bi, i: (bi, jnp.maximum(i * hb - 1, 0), 0)),
            pl.BlockSpec((1, ts, c), lambda bi, i: (bi, i, 0)),
            pl.BlockSpec((1, HALO, c), lambda bi, i: (bi, jnp.minimum((i + 1) * hb, n_hb - 1), 0)),
        ]

    w_list = [wts["wo"], wts["g2"], wts["wup"], wts["conv_f"], wts["wdn"], wts["gf"]]
    return pl.pallas_call(
        kern,
        grid=(b, n_i),
        in_specs=three(D_MODEL) + three(A_W) + three(MLA_W)
        + [pl.BlockSpec((1, 6, D_MODEL), lambda bi, i: (bi, 0, 0))]
        + [_const_spec(w.shape) for w in w_list],
        out_specs=pl.BlockSpec((1, ts, D_MODEL), lambda bi, i: (bi, i, 0)),
        out_shape=jax.ShapeDtypeStruct((b, s, D_MODEL), F32),
        scratch_shapes=[pltpu.VMEM((ts + 2 * HALO, D_MODEL), BF16),
                        pltpu.VMEM((ts + 2 * HALO, 2 * FF_CHUNK), F32),
                        pltpu.VMEM((ts, D_MODEL), F32)],
        compiler_params=pltpu.CompilerParams(dimension_semantics=("parallel", "arbitrary"),
                                             vmem_limit_bytes=VMEM_LIMIT),
        name="post_mlp",
    )(x, x, x, ya, ya, ya, yb, yb, yb, mod, *w_list)


def _rot_half(w, axis):
    x1, x2 = jnp.split(w, 2, axis=axis)
    return jnp.concatenate([-x2, x1], axis=axis)


def _prep_weights(norm1_g, w_in, conv_a_w, q_norm_g, w_uq, kv_norm_g, w_ukv, out_norm_a_g, out_norm_b_g,
                  w_o, norm2_g, w_up, ffn_conv_w, w_down, final_g):
    cuts = np.cumsum([A_W, A_W, A_W, Q_LORA, KV_LORA]).tolist()
    w_h, w_b, w_c, w_cq, w_ckv, w_kr = jnp.split(w_in, cuts, axis=1)
    w1 = jnp.concatenate([w_h, w_c], axis=1)
    w2 = jnp.concatenate([w_b, w_cq, w_ckv, w_kr, _rot_half(w_kr, 1)], axis=1)

    wq3 = w_uq.reshape(Q_LORA, N_HEADS, QK_NOPE + QK_ROPE)
    wq3 = jnp.concatenate([wq3, _rot_half(wq3[:, :, QK_NOPE:], 2)], axis=2)
    wqt = wq3.reshape(Q_LORA, N_HEADS * HEAD_W).T

    wkv3 = w_ukv.reshape(KV_LORA, N_HEADS, QK_NOPE + V_DIM)
    wk = wkv3[:, :, :QK_NOPE].reshape(KV_LORA, N_HEADS * QK_NOPE)
    wvt = wkv3[:, :, QK_NOPE:].reshape(KV_LORA, N_HEADS * V_DIM).T

    wup = w_up.reshape(D_MODEL, 2, N_FF_CHUNKS, FF_CHUNK).transpose(2, 0, 1, 3).reshape(
        N_FF_CHUNKS, D_MODEL, 2 * FF_CHUNK)
    conv_f = ffn_conv_w.reshape(3, 2, N_FF_CHUNKS, FF_CHUNK).transpose(2, 0, 1, 3).reshape(
        N_FF_CHUNKS, 3, 2 * FF_CHUNK)
    wdn = w_down.reshape(N_FF_CHUNKS, FF_CHUNK, D_MODEL)
    row = lambda g: g.reshape(1, -1)
    return {
        "g1": row(norm1_g), "w1": w1.astype(BF16), "w2": w2.astype(BF16), "conv_a": conv_a_w,
        "gq": row(q_norm_g), "wqt": wqt.astype(BF16), "gkv": row(kv_norm_g), "wk": wk.astype(BF16),
        "wvt": wvt.astype(BF16), "ga": row(out_norm_a_g), "gb": row(out_norm_b_g), "wo": w_o.astype(BF16),
        "g2": row(norm2_g), "wup": wup.astype(BF16), "conv_f": conv_f, "wdn": wdn.astype(BF16),
        "gf": row(final_g),
    }


def _rope_table(s):
    inv = 1.0 / (ROPE_THETA ** (jnp.arange(0, QK_ROPE, 2, dtype=F32) / QK_ROPE))
    ang = jnp.arange(s, dtype=F32)[:, None] * inv[None, :]
    cos, sin = jnp.cos(ang), jnp.sin(ang)
    return jnp.concatenate([cos, cos, sin, sin], axis=1)


def _tiles(s):
    return min(512, s), min(512, s)


def _forward(x, mod, wts):
    s = x.shape[1]
    ts, tq = _tiles(s)
    ya, qt, k, vt = _pre_call(x, mod, _rope_table(s), wts, ts=ts)
    yb = _attn_call(qt, k, vt, wts["gb"], tq=tq)
    return _post_call(x, ya, yb, mod, wts, ts=ts)


def kernel(x_prompt, x_sample, c_prompt, c_sample, w_ada, b_ada, norm1_g, w_in, conv_a_w, q_norm_g, w_uq,
           kv_norm_g, w_ukv, out_norm_a_g, out_norm_b_g, w_o, norm2_g, w_up, ffn_conv_w, w_down, final_g):
    assert w_ada.shape[0] == 1, "single-layer kernel"
    wts = _prep_weights(norm1_g[0], w_in[0], conv_a_w[0], q_norm_g[0], w_uq[0], kv_norm_g[0], w_ukv[0],
                        out_norm_a_g[0], out_norm_b_g[0], w_o[0], norm2_g[0], w_up[0], ffn_conv_w[0],
                        w_down[0], final_g)
    nbp = c_prompt.shape[0]
    c_all = jnp.concatenate([c_prompt, c_sample], axis=0)
    mod = _mod_call(c_all, w_ada[0].astype(BF16), b_ada[0].reshape(1, -1))
    mod = mod.reshape(c_all.shape[0], 6, D_MODEL)
    y_prompt = _forward(x_prompt, mod[:nbp], wts)
    y_sample = _forward(x_sample, mod[nbp:], wts)
    return (y_prompt, y_sample)
```

```python
import functools
import math

import jax
import jax.numpy as jnp
import numpy as np
from jax import lax
from jax.experimental import pallas as pl
from jax.experimental.pallas import tpu as pltpu

D_MODEL = 1024
A_W = 512
N_HEADS = 4
QK_NOPE = 128
QK_ROPE = 64
V_DIM = 128
Q_LORA = 384
KV_LORA = 256
MLA_W = N_HEADS * V_DIM
D_FF = 2816
ROPE_THETA = 10000.0
ATTN_SCALE = 1.0 / math.sqrt(QK_NOPE + QK_ROPE)
Q_PRESCALE = ATTN_SCALE * math.log2(math.e)
EPS = 1e-6

LANES = 128
HEAD_W = 2 * LANES
HALO = 16
SUM_ROWS = 16
FF_CHUNK = 256
N_FF_CHUNKS = D_FF // FF_CHUNK
VMEM_LIMIT = 56 * 1024 * 1024

BF16 = jnp.bfloat16
F32 = jnp.float32


def _rms(x, g):
    return x * lax.rsqrt(jnp.mean(x * x, axis=-1, keepdims=True) + EPS) * g


def _dot(a, b):
    return jnp.dot(a, b, preferred_element_type=F32)


def _dot_nt(a, b):
    return lax.dot_general(a, b, (((1,), (1,)), ((), ())), preferred_element_type=F32)


def _const_spec(shape):
    nd = len(shape)
    return pl.BlockSpec(shape, lambda *_: (0,) * nd, pipeline_mode=pl.Buffered(1))


def _mod_kernel(c_ref, w_ref, b_ref, o_ref):
    c = c_ref[...]
    act = c / (1.0 + jnp.exp(-c))
    o_ref[...] = _dot(act.astype(BF16), w_ref[...]) + b_ref[...]


def _mod_call(c_all, w_ada, b_ada):
    nb = c_all.shape[0]
    n_out = w_ada.shape[1]
    bn = D_MODEL
    return pl.pallas_call(
        _mod_kernel,
        grid=(n_out // bn,),
        in_specs=[
            pl.BlockSpec((nb, D_MODEL), lambda j: (0, 0)),
            pl.BlockSpec((D_MODEL, bn), lambda j: (0, j)),
            pl.BlockSpec((1, bn), lambda j: (0, j)),
        ],
        out_specs=pl.BlockSpec((nb, bn), lambda j: (0, j)),
        out_shape=jax.ShapeDtypeStruct((nb, n_out), F32),
        name="adaln_mod",
    )(c_all, w_ada, b_ada)


def _pre_kernel(xp_ref, xm_ref, xn_ref, mod_ref, cs_ref, cst_ref, g1_ref, w1_ref, w2_ref, cw_ref,
                gq_ref, wqt_ref, gkv_ref, wk_ref, wvt_ref, ga_ref,
                ya_ref, qt_ref, k_ref, vt_ref,
                h_buf, p_buf, *, ts):
    i = pl.program_id(1)
    n_i = pl.num_programs(1)
    g1 = g1_ref[...]
    scale = 1.0 + mod_ref[0, 1:2, :]
    shift = mod_ref[0, 0:1, :]

    def norm_mod(x):
        return (_rms(x, g1) * scale + shift).astype(BF16)

    h_buf[0:HALO, :] = norm_mod(xp_ref[0])
    h_buf[HALO:HALO + ts, :] = norm_mod(xm_ref[0])
    h_buf[HALO + ts:, :] = norm_mod(xn_ref[0])

    z1 = _dot(h_buf[...], w1_ref[...])
    p = z1[:, :A_W] * z1[:, A_W:]
    p_buf[...] = p
    p_buf[0:HALO, :] = jnp.where(i > 0, p[0:HALO], 0.0)
    p_buf[HALO + ts:, :] = jnp.where(i < n_i - 1, p[HALO + ts:], 0.0)
    cw = cw_ref[...]
    conv = (p_buf[HALO - 1:HALO - 1 + ts, :] * cw[0:1]
            + p_buf[HALO:HALO + ts, :] * cw[1:2]
            + p_buf[HALO + 1:HALO + 1 + ts, :] * cw[2:3])

    z2 = _dot(h_buf[HALO:HALO + ts, :], w2_ref[...])
    y_a = z2[:, :A_W] * conv
    ya_ref[0] = _rms(y_a, ga_ref[...]).astype(BF16)

    cs = cs_ref[...]
    keep = lax.broadcasted_iota(jnp.int32, (1, LANES), 1) < QK_ROPE
    o = A_W
    c_q = z2[:, o:o + Q_LORA]
    o += Q_LORA
    c_kv = z2[:, o:o + KV_LORA]
    o += KV_LORA
    t = z2[:, o:o + LANES] * cs
    k_rope = jnp.where(keep, t + pltpu.roll(t, QK_ROPE, axis=1), 0.0).astype(BF16)

    cqn = _rms(c_q, gq_ref[...]).astype(BF16)
    ckvn = _rms(c_kv, gkv_ref[...]).astype(BF16)
    qt = _dot_nt(wqt_ref[...], cqn)
    k_nope = _dot(ckvn, wk_ref[...])
    vt = _dot_nt(wvt_ref[...], ckvn)
    cos_t = cst_ref[0:QK_ROPE, :]
    sin_t = cst_ref[QK_ROPE:, :]
    for h in range(N_HEADS):
        b0 = h * HEAD_W
        r0 = b0 + QK_NOPE
        qt_ref[0, h, 0:QK_NOPE, :] = (qt[b0:r0] * Q_PRESCALE).astype(BF16)
        q_rope = qt[r0:r0 + QK_ROPE] * cos_t + qt[r0 + QK_ROPE:b0 + HEAD_W] * sin_t
        qt_ref[0, h, QK_NOPE:QK_NOPE + QK_ROPE, :] = (q_rope * Q_PRESCALE).astype(BF16)
        qt_ref[0, h, QK_NOPE + QK_ROPE:, :] = jnp.zeros((HEAD_W - QK_NOPE - QK_ROPE, ts), BF16)
        k_ref[0, h, :, 0:LANES] = k_nope[:, h * QK_NOPE:(h + 1) * QK_NOPE].astype(BF16)
        k_ref[0, h, :, LANES:] = k_rope
        vt_ref[0, h, 0] = vt[h * V_DIM:(h + 1) * V_DIM].astype(BF16)


def _pre_call(x, mod, cs, wts, *, ts):
    b, s, _ = x.shape
    n_i = s // ts
    hb = ts // HALO
    n_hb = s // HALO
    kern = functools.partial(_pre_kernel, ts=ts)
    w_list = [wts["g1"], wts["w1"], wts["w2"], wts["conv_a"], wts["gq"], wts["wqt"], wts["gkv"], wts["wk"],
              wts["wvt"], wts["ga"]]
    return pl.pallas_call(
        kern,
        grid=(b, n_i),
        in_specs=[
            pl.BlockSpec((1, HALO, D_MODEL), lambda bi, i: (bi, jnp.maximum(i * hb - 1, 0), 0)),
            pl.BlockSpec((1, ts, D_MODEL), lambda bi, i: (bi, i, 0)),
            pl.BlockSpec((1, HALO, D_MODEL), lambda bi, i: (bi, jnp.minimum((i + 1) * hb, n_hb - 1), 0)),
            pl.BlockSpec((1, 6, D_MODEL), lambda bi, i: (bi, 0, 0)),
            pl.BlockSpec((ts, LANES), lambda bi, i: (i, 0)),
            pl.BlockSpec((LANES, ts), lambda bi, i: (0, i)),
        ] + [_const_spec(w.shape) for w in w_list],
        out_specs=[
            pl.BlockSpec((1, ts, A_W), lambda bi, i: (bi, i, 0)),
            pl.BlockSpec((1, N_HEADS, HEAD_W, ts), lambda bi, i: (bi, 0, 0, i)),
            pl.BlockSpec((1, N_HEADS, ts, HEAD_W), lambda bi, i: (bi, 0, i, 0)),
            pl.BlockSpec((1, N_HEADS, 1, V_DIM, ts), lambda bi, i: (bi, 0, i, 0, 0)),
        ],
        out_shape=[
            jax.ShapeDtypeStruct((b, s, A_W), BF16),
            jax.ShapeDtypeStruct((b, N_HEADS, HEAD_W, s), BF16),
            jax.ShapeDtypeStruct((b, N_HEADS, s, HEAD_W), BF16),
            jax.ShapeDtypeStruct((b, N_HEADS, n_i, V_DIM, ts), BF16),
        ],
        scratch_shapes=[pltpu.VMEM((ts + 2 * HALO, D_MODEL), BF16),
                        pltpu.VMEM((ts + 2 * HALO, A_W), F32)],
        compiler_params=pltpu.CompilerParams(dimension_semantics=("parallel", "arbitrary"),
                                             vmem_limit_bytes=VMEM_LIMIT),
        name="pre_attn",
    )(x, x, x, mod, cs, cs.T, *w_list)


def _attn_kernel(qt_ref, k_ref, vt_ref, gb_ref, o_ref, s_buf, acc_buf):
    n_chunks, _, tk = vt_ref.shape[2:]
    tq = qt_ref.shape[3]
    n_items = N_HEADS * n_chunks
    ones = jnp.ones((SUM_ROWS, tk), BF16)

    def scores(n, slot):
        h = lax.div(n, n_chunks)
        k0 = pl.multiple_of(lax.rem(n, n_chunks) * tk, tk)
        st = _dot(k_ref[0, h, pl.ds(k0, tk), :], qt_ref[0, h])
        s_buf[slot] = st
        return jnp.max(st, axis=0, keepdims=True)

    def accumulate(n, slot, cmax, m, acc):
        h = lax.div(n, n_chunks)
        j = lax.rem(n, n_chunks)
        m = jnp.where(j == 0, -jnp.inf, m)
        m_new = jnp.maximum(m, cmax)
        alpha = jnp.exp2(m - m_new)
        pt = jnp.exp2(s_buf[slot] - m_new).astype(BF16)
        v_ext = jnp.concatenate([vt_ref[0, h, j], ones], axis=0)
        acc = acc * alpha + _dot(v_ext, pt)
        acc_buf[h] = acc
        return m_new, acc

    def pair(nn, carry):
        m, acc, cmax0 = carry
        n = 2 * nn
        cmax1 = scores(n + 1, 1)
        m, acc = accumulate(n, 0, cmax0, m, acc)
        cmax0 = scores(jnp.minimum(n + 2, n_items - 1), 0)
        m, acc = accumulate(n + 1, 1, cmax1, m, acc)
        return m, acc, cmax0

    init = (jnp.full((1, tq), -jnp.inf, F32), jnp.zeros((V_DIM + SUM_ROWS, tq), F32), scores(0, 0))
    lax.fori_loop(0, n_items // 2, pair, init)
    heads = []
    for h in range(N_HEADS):
        acc = acc_buf[h]
        heads.append((acc[:V_DIM] / acc[V_DIM:V_DIM + 1]).T)
    y_b = jnp.concatenate(heads, axis=1)
    o_ref[0] = _rms(y_b, gb_ref[...]).astype(BF16)


def _attn_call(qt, k, vt, gb, *, tq):
    b, _, _, s = qt.shape
    n_chunks, _, tk = vt.shape[2:]
    resident = pl.Buffered(1)
    return pl.pallas_call(
        _attn_kernel,
        grid=(b, s // tq),
        in_specs=[
            pl.BlockSpec((1, N_HEADS, HEAD_W, tq), lambda bi, i: (bi, 0, 0, i)),
            pl.BlockSpec((1, N_HEADS, s, HEAD_W), lambda bi, i: (bi, 0, 0, 0), pipeline_mode=resident),
            pl.BlockSpec((1, N_HEADS, n_chunks, V_DIM, tk), lambda bi, i: (bi, 0, 0, 0, 0),
                         pipeline_mode=resident),
            _const_spec(gb.shape),
        ],
        out_specs=pl.BlockSpec((1, tq, MLA_W), lambda bi, i: (bi, i, 0)),
        out_shape=jax.ShapeDtypeStruct((b, s, MLA_W), BF16),
        scratch_shapes=[pltpu.VMEM((2, tk, tq), F32),
                        pltpu.VMEM((N_HEADS, V_DIM + SUM_ROWS, tq), F32)],
        compiler_params=pltpu.CompilerParams(dimension_semantics=("parallel", "arbitrary"),
                                             vmem_limit_bytes=VMEM_LIMIT),
        name="mla_attn",
    )(qt, k, vt, gb)


def _post_kernel(xp_ref, xm_ref, xn_ref, yap_ref, yam_ref, yan_ref, ybp_ref, ybm_ref, ybn_ref, mod_ref,
                 wo_ref, g2_ref, wup_ref, cw_ref, wdn_ref, gf_ref,
                 o_ref,
                 h_buf, a_buf, acc_buf, *, ts):
    i = pl.program_id(1)
    n_i = pl.num_programs(1)
    g2 = g2_ref[...]
    wo = wo_ref[...]
    gate1 = mod_ref[0, 2:3, :]
    shift2 = mod_ref[0, 3:4, :]
    scale2 = 1.0 + mod_ref[0, 4:5, :]
    gate2 = mod_ref[0, 5:6, :]

    def mix_residual(x, ya, yb):
        cat = jnp.concatenate([ya, yb], axis=-1)
        return x + gate1 * _dot(cat, wo)

    def norm_mod(x1):
        return _rms(x1, g2) * scale2 + shift2

    hp = norm_mod(mix_residual(xp_ref[0], yap_ref[0], ybp_ref[0]))
    h_buf[0:HALO, :] = jnp.where(i > 0, hp, 0.0).astype(BF16)
    x1 = mix_residual(xm_ref[0], yam_ref[0], ybm_ref[0])
    h_buf[HALO:HALO + ts, :] = norm_mod(x1).astype(BF16)
    hn = norm_mod(mix_residual(xn_ref[0], yan_ref[0], ybn_ref[0]))
    h_buf[HALO + ts:, :] = jnp.where(i < n_i - 1, hn, 0.0).astype(BF16)

    acc_buf[...] = jnp.zeros_like(acc_buf)

    def up_proj(c, slot):
        a_buf[slot] = _dot(h_buf[...], wup_ref[c])

    def gate_down(c, slot):
        cw = cw_ref[c]
        u = (a_buf[slot, HALO - 1:HALO - 1 + ts, :] * cw[0:1]
             + a_buf[slot, HALO:HALO + ts, :] * cw[1:2]
             + a_buf[slot, HALO + 1:HALO + 1 + ts, :] * cw[2:3])
        g = u[:, :FF_CHUNK]
        act = g / (1.0 + jnp.exp(-g)) * u[:, FF_CHUNK:]
        acc_buf[...] += _dot(act.astype(BF16), wdn_ref[c])

    def pair(cc, carry):
        c = 2 * cc
        up_proj(c + 1, 1)
        gate_down(c, 0)
        up_proj(c + 2, 0)
        gate_down(c + 1, 1)
        return carry

    assert N_FF_CHUNKS % 2 == 1
    up_proj(0, 0)
    lax.fori_loop(0, N_FF_CHUNKS // 2, pair, 0)
    gate_down(N_FF_CHUNKS - 1, 0)
    x2 = x1 + gate2 * acc_buf[...]
    o_ref[0] = _rms(x2, gf_ref[...])


def _post_call(x, ya, yb, mod, wts, *, ts):
    b, s, _ = x.shape
    n_i = s // ts
    hb = ts // HALO
    n_hb = s // HALO
    kern = functools.partial(_post_kernel, ts=ts)

    def three(c):
        return [
            pl.BlockSpec((1, HALO, c), lambda bi, i: (bi, jnp.maximum(i * hb - 1, 0), 0)),
            pl.BlockSpec((1, ts, c), lambda bi, i: (bi, i, 0)),
            pl.BlockSpec((1, HALO, c), lambda bi, i: (bi, jnp.minimum((i + 1) * hb, n_hb - 1), 0)),
        ]

    w_list = [wts["wo"], wts["g2"], wts["wup"], wts["conv_f"], wts["wdn"], wts["gf"]]
    return pl.pallas_call(
        kern,
        grid=(b, n_i),
        in_specs=three(D_MODEL) + three(A_W) + three(MLA_W)
        + [pl.BlockSpec((1, 6, D_MODEL), lambda bi, i: (bi, 0, 0))]
        + [_const_spec(w.shape) for w in w_list],
        out_specs=pl.BlockSpec((1, ts, D_MODEL), lambda bi, i: (bi, i, 0)),
        out_shape=jax.ShapeDtypeStruct((b, s, D_MODEL), F32),
        scratch_shapes=[pltpu.VMEM((ts + 2 * HALO, D_MODEL), BF16),
                        pltpu.VMEM((2, ts + 2 * HALO, 2 * FF_CHUNK), F32),
                        pltpu.VMEM((ts, D_MODEL), F32)],
        compiler_params=pltpu.CompilerParams(dimension_semantics=("parallel", "arbitrary"),
                                             vmem_limit_bytes=VMEM_LIMIT),
        name="post_mlp",
    )(x, x, x, ya, ya, ya, yb, yb, yb, mod, *w_list)


def _rot_half(w, axis):
    x1, x2 = jnp.split(w, 2, axis=axis)
    return jnp.concatenate([-x2, x1], axis=axis)


def _prep_weights(norm1_g, w_in, conv_a_w, q_norm_g, w_uq, kv_norm_g, w_ukv, out_norm_a_g, out_norm_b_g,
                  w_o, norm2_g, w_up, ffn_conv_w, w_down, final_g):
    cuts = np.cumsum([A_W, A_W, A_W, Q_LORA, KV_LORA]).tolist()
    w_h, w_b, w_c, w_cq, w_ckv, w_kr = jnp.split(w_in, cuts, axis=1)
    w1 = jnp.concatenate([w_h, w_c], axis=1)
    w2 = jnp.concatenate([w_b, w_cq, w_ckv, w_kr, _rot_half(w_kr, 1)], axis=1)

    wq3 = w_uq.reshape(Q_LORA, N_HEADS, QK_NOPE + QK_ROPE)
    wq3 = jnp.concatenate([wq3, _rot_half(wq3[:, :, QK_NOPE:], 2)], axis=2)
    wqt = wq3.reshape(Q_LORA, N_HEADS * HEAD_W).T

    wkv3 = w_ukv.reshape(KV_LORA, N_HEADS, QK_NOPE + V_DIM)
    wk = wkv3[:, :, :QK_NOPE].reshape(KV_LORA, N_HEADS * QK_NOPE)
    wvt = wkv3[:, :, QK_NOPE:].reshape(KV_LORA, N_HEADS * V_DIM).T

    wup = w_up.reshape(D_MODEL, 2, N_FF_CHUNKS, FF_CHUNK).transpose(2, 0, 1, 3).reshape(
        N_FF_CHUNKS, D_MODEL, 2 * FF_CHUNK)
    conv_f = ffn_conv_w.reshape(3, 2, N_FF_CHUNKS, FF_CHUNK).transpose(2, 0, 1, 3).reshape(
        N_FF_CHUNKS, 3, 2 * FF_CHUNK)
    wdn = w_down.reshape(N_FF_CHUNKS, FF_CHUNK, D_MODEL)
    row = lambda g: g.reshape(1, -1)
    return {
        "g1": row(norm1_g), "w1": w1.astype(BF16), "w2": w2.astype(BF16), "conv_a": conv_a_w,
        "gq": row(q_norm_g), "wqt": wqt.astype(BF16), "gkv": row(kv_norm_g), "wk": wk.astype(BF16),
        "wvt": wvt.astype(BF16), "ga": row(out_norm_a_g), "gb": row(out_norm_b_g), "wo": w_o.astype(BF16),
        "g2": row(norm2_g), "wup": wup.astype(BF16), "conv_f": conv_f, "wdn": wdn.astype(BF16),
        "gf": row(final_g),
    }


def _rope_table(s):
    inv = 1.0 / (ROPE_THETA ** (jnp.arange(0, QK_ROPE, 2, dtype=F32) / QK_ROPE))
    ang = jnp.arange(s, dtype=F32)[:, None] * inv[None, :]
    cos, sin = jnp.cos(ang), jnp.sin(ang)
    return jnp.concatenate([cos, cos, sin, sin], axis=1)


def _tiles(s):
    return min(512, s), min(512, s)


def _forward(x, mod, wts):
    s = x.shape[1]
    ts, tq = _tiles(s)
    ya, qt, k, vt = _pre_call(x, mod, _rope_table(s), wts, ts=ts)
    yb = _attn_call(qt, k, vt, wts["gb"], tq=tq)
    return _post_call(x, ya, yb, mod, wts, ts=ts)


def kernel(x_prompt, x_sample, c_prompt, c_sample, w_ada, b_ada, norm1_g, w_in, conv_a_w, q_norm_g, w_uq,
           kv_norm_g, w_ukv, out_norm_a_g, out_norm_b_g, w_o, norm2_g, w_up, ffn_conv_w, w_down, final_g):
    assert w_ada.shape[0] == 1, "single-layer kernel"
    wts = _prep_weights(norm1_g[0], w_in[0], conv_a_w[0], q_norm_g[0], w_uq[0], kv_norm_g[0], w_ukv[0],
                        out_norm_a_g[0], out_norm_b_g[0], w_o[0], norm2_g[0], w_up[0], ffn_conv_w[0],
                        w_down[0], final_g)
    nbp = c_prompt.shape[0]
    c_all = jnp.concatenate([c_prompt, c_sample], axis=0)
    mod = _mod_call(c_all, w_ada[0].astype(BF16), b_ada[0].reshape(1, -1))
    mod = mod.reshape(c_all.shape[0], 6, D_MODEL)
    y_prompt = _forward(x_prompt, mod[:nbp], wts)
    y_sample = _forward(x_sample, mod[nbp:], wts)
    return (y_prompt, y_sample)
```

```python
import functools
import math

import jax
import jax.numpy as jnp
import numpy as np
from jax import lax
from jax.experimental import pallas as pl
from jax.experimental.pallas import tpu as pltpu

D_MODEL = 1024
A_W = 512
N_HEADS = 4
QK_NOPE = 128
QK_ROPE = 64
V_DIM = 128
Q_LORA = 384
KV_LORA = 256
MLA_W = N_HEADS * V_DIM
D_FF = 2816
ROPE_THETA = 10000.0
ATTN_SCALE = 1.0 / math.sqrt(QK_NOPE + QK_ROPE)
Q_PRESCALE = ATTN_SCALE * math.log2(math.e)
EPS = 1e-6

LANES = 128
HEAD_W = 2 * LANES
HALO = 16
SUM_ROWS = 16
ATTN_UNROLL = 8
FF_CHUNK = 256
N_FF_CHUNKS = D_FF // FF_CHUNK
VMEM_LIMIT = 56 * 1024 * 1024

BF16 = jnp.bfloat16
F32 = jnp.float32


def _rms(x, g):
    return x * lax.rsqrt(jnp.mean(x * x, axis=-1, keepdims=True) + EPS) * g


def _dot(a, b):
    return jnp.dot(a, b, preferred_element_type=F32)


def _dot_nt(a, b):
    return lax.dot_general(a, b, (((1,), (1,)), ((), ())), preferred_element_type=F32)


def _const_spec(shape):
    nd = len(shape)
    return pl.BlockSpec(shape, lambda *_: (0,) * nd, pipeline_mode=pl.Buffered(1))


def _mod_kernel(c_ref, w_ref, b_ref, o_ref):
    c = c_ref[...]
    act = c / (1.0 + jnp.exp(-c))
    o_ref[...] = _dot(act.astype(BF16), w_ref[...]) + b_ref[...]


def _mod_call(c_all, w_ada, b_ada):
    nb = c_all.shape[0]
    n_out = w_ada.shape[1]
    bn = D_MODEL
    return pl.pallas_call(
        _mod_kernel,
        grid=(n_out // bn,),
        in_specs=[
            pl.BlockSpec((nb, D_MODEL), lambda j: (0, 0)),
            pl.BlockSpec((D_MODEL, bn), lambda j: (0, j)),
            pl.BlockSpec((1, bn), lambda j: (0, j)),
        ],
        out_specs=pl.BlockSpec((nb, bn), lambda j: (0, j)),
        out_shape=jax.ShapeDtypeStruct((nb, n_out), F32),
        name="adaln_mod",
    )(c_all, w_ada, b_ada)


def _pre_kernel(xp_ref, xm_ref, xn_ref, mod_ref, cs_ref, cst_ref, g1_ref, w1_ref, w2_ref, cw_ref,
                gq_ref, wqt_ref, gkv_ref, wk_ref, wvt_ref, ga_ref,
                ya_ref, qt_ref, k_ref, vt_ref,
                h_buf, p_buf, *, ts):
    i = pl.program_id(1)
    n_i = pl.num_programs(1)
    g1 = g1_ref[...]
    scale = 1.0 + mod_ref[0, 1:2, :]
    shift = mod_ref[0, 0:1, :]

    def norm_mod(x):
        return (_rms(x, g1) * scale + shift).astype(BF16)

    h_buf[0:HALO, :] = norm_mod(xp_ref[0])
    h_buf[HALO:HALO + ts, :] = norm_mod(xm_ref[0])
    h_buf[HALO + ts:, :] = norm_mod(xn_ref[0])

    z1 = _dot(h_buf[...], w1_ref[...])
    p = z1[:, :A_W] * z1[:, A_W:]
    p_buf[...] = p
    p_buf[0:HALO, :] = jnp.where(i > 0, p[0:HALO], 0.0)
    p_buf[HALO + ts:, :] = jnp.where(i < n_i - 1, p[HALO + ts:], 0.0)
    cw = cw_ref[...]
    conv = (p_buf[HALO - 1:HALO - 1 + ts, :] * cw[0:1]
            + p_buf[HALO:HALO + ts, :] * cw[1:2]
            + p_buf[HALO + 1:HALO + 1 + ts, :] * cw[2:3])

    z2 = _dot(h_buf[HALO:HALO + ts, :], w2_ref[...])
    y_a = z2[:, :A_W] * conv
    ya_ref[0] = _rms(y_a, ga_ref[...]).astype(BF16)

    cs = cs_ref[...]
    keep = lax.broadcasted_iota(jnp.int32, (1, LANES), 1) < QK_ROPE
    o = A_W
    c_q = z2[:, o:o + Q_LORA]
    o += Q_LORA
    c_kv = z2[:, o:o + KV_LORA]
    o += KV_LORA
    t = z2[:, o:o + LANES] * cs
    k_rope = jnp.where(keep, t + pltpu.roll(t, QK_ROPE, axis=1), 0.0).astype(BF16)

    cqn = _rms(c_q, gq_ref[...]).astype(BF16)
    ckvn = _rms(c_kv, gkv_ref[...]).astype(BF16)
    qt = _dot_nt(wqt_ref[...], cqn)
    k_nope = _dot(ckvn, wk_ref[...])
    vt = _dot_nt(wvt_ref[...], ckvn)
    cos_t = cst_ref[0:QK_ROPE, :]
    sin_t = cst_ref[QK_ROPE:, :]
    for h in range(N_HEADS):
        b0 = h * HEAD_W
        r0 = b0 + QK_NOPE
        qt_ref[0, h, 0:QK_NOPE, :] = (qt[b0:r0] * Q_PRESCALE).astype(BF16)
        q_rope = qt[r0:r0 + QK_ROPE] * cos_t + qt[r0 + QK_ROPE:b0 + HEAD_W] * sin_t
        qt_ref[0, h, QK_NOPE:QK_NOPE + QK_ROPE, :] = (q_rope * Q_PRESCALE).astype(BF16)
        qt_ref[0, h, QK_NOPE + QK_ROPE:, :] = jnp.zeros((HEAD_W - QK_NOPE - QK_ROPE, ts), BF16)
        k_ref[0, h, :, 0:LANES] = k_nope[:, h * QK_NOPE:(h + 1) * QK_NOPE].astype(BF16)
        k_ref[0, h, :, LANES:] = k_rope
        vt_ref[0, h, 0] = vt[h * V_DIM:(h + 1) * V_DIM].astype(BF16)


def _pre_call(x, mod, cs, wts, *, ts):
    b, s, _ = x.shape
    n_i = s // ts
    hb = ts // HALO
    n_hb = s // HALO
    kern = functools.partial(_pre_kernel, ts=ts)
    w_list = [wts["g1"], wts["w1"], wts["w2"], wts["conv_a"], wts["gq"], wts["wqt"], wts["gkv"], wts["wk"],
              wts["wvt"], wts["ga"]]
    return pl.pallas_call(
        kern,
        grid=(b, n_i),
        in_specs=[
            pl.BlockSpec((1, HALO, D_MODEL), lambda bi, i: (bi, jnp.maximum(i * hb - 1, 0), 0)),
            pl.BlockSpec((1, ts, D_MODEL), lambda bi, i: (bi, i, 0)),
            pl.BlockSpec((1, HALO, D_MODEL), lambda bi, i: (bi, jnp.minimum((i + 1) * hb, n_hb - 1), 0)),
            pl.BlockSpec((1, 6, D_MODEL), lambda bi, i: (bi, 0, 0)),
            pl.BlockSpec((ts, LANES), lambda bi, i: (i, 0)),
            pl.BlockSpec((LANES, ts), lambda bi, i: (0, i)),
        ] + [_const_spec(w.shape) for w in w_list],
        out_specs=[
            pl.BlockSpec((1, ts, A_W), lambda bi, i: (bi, i, 0)),
            pl.BlockSpec((1, N_HEADS, HEAD_W, ts), lambda bi, i: (bi, 0, 0, i)),
            pl.BlockSpec((1, N_HEADS, ts, HEAD_W), lambda bi, i: (bi, 0, i, 0)),
            pl.BlockSpec((1, N_HEADS, 1, V_DIM, ts), lambda bi, i: (bi, 0, i, 0, 0)),
        ],
        out_shape=[
            jax.ShapeDtypeStruct((b, s, A_W), BF16),
            jax.ShapeDtypeStruct((b, N_HEADS, HEAD_W, s), BF16),
            jax.ShapeDtypeStruct((b, N_HEADS, s, HEAD_W), BF16),
            jax.ShapeDtypeStruct((b, N_HEADS, n_i, V_DIM, ts), BF16),
        ],
        scratch_shapes=[pltpu.VMEM((ts + 2 * HALO, D_MODEL), BF16),
                        pltpu.VMEM((ts + 2 * HALO, A_W), F32)],
        compiler_params=pltpu.CompilerParams(dimension_semantics=("parallel", "arbitrary"),
                                             vmem_limit_bytes=VMEM_LIMIT),
        name="pre_attn",
    )(x, x, x, mod, cs, cs.T, *w_list)


def _attn_kernel(qt_ref, k_ref, vt_ref, gb_ref, o_ref, s_buf, acc_buf):
    n_chunks, _, tk = vt_ref.shape[2:]
    tq = qt_ref.shape[3]
    n_items = N_HEADS * n_chunks
    ones = jnp.ones((SUM_ROWS, tk), BF16)

    def scores(n, slot):
        h = lax.div(n, n_chunks)
        k0 = pl.multiple_of(lax.rem(n, n_chunks) * tk, tk)
        st = _dot(k_ref[0, h, pl.ds(k0, tk), :], qt_ref[0, h])
        s_buf[slot] = st
        return jnp.max(st, axis=0, keepdims=True)

    def accumulate(n, slot, cmax, m, acc):
        h = lax.div(n, n_chunks)
        j = lax.rem(n, n_chunks)
        m = jnp.where(j == 0, -jnp.inf, m)
        m_new = jnp.maximum(m, cmax)
        alpha = jnp.exp2(m - m_new)
        pt = jnp.exp2(s_buf[slot] - m_new).astype(BF16)
        v_ext = jnp.concatenate([vt_ref[0, h, j], ones], axis=0)
        acc = acc * alpha + _dot(v_ext, pt)
        acc_buf[h] = acc
        return m_new, acc

    def group(nn, carry):
        m, acc, cmax = carry
        for u in range(ATTN_UNROLL):
            n = ATTN_UNROLL * nn + u
            slot = u % 2
            nxt = n + 1 if u < ATTN_UNROLL - 1 else jnp.minimum(n + 1, n_items - 1)
            cmax_next = scores(nxt, 1 - slot)
            m, acc = accumulate(n, slot, cmax, m, acc)
            cmax = cmax_next
        return m, acc, cmax

    init = (jnp.full((1, tq), -jnp.inf, F32), jnp.zeros((V_DIM + SUM_ROWS, tq), F32), scores(0, 0))
    lax.fori_loop(0, n_items // ATTN_UNROLL, group, init)
    heads = []
    for h in range(N_HEADS):
        acc = acc_buf[h]
        heads.append((acc[:V_DIM] / acc[V_DIM:V_DIM + 1]).T)
    y_b = jnp.concatenate(heads, axis=1)
    o_ref[0] = _rms(y_b, gb_ref[...]).astype(BF16)


def _attn_call(qt, k, vt, gb, *, tq):
    b, _, _, s = qt.shape
    n_chunks, _, tk = vt.shape[2:]
    resident = pl.Buffered(1)
    return pl.pallas_call(
        _attn_kernel,
        grid=(b, s // tq),
        in_specs=[
            pl.BlockSpec((1, N_HEADS, HEAD_W, tq), lambda bi, i: (bi, 0, 0, i)),
            pl.BlockSpec((1, N_HEADS, s, HEAD_W), lambda bi, i: (bi, 0, 0, 0), pipeline_mode=resident),
            pl.BlockSpec((1, N_HEADS, n_chunks, V_DIM, tk), lambda bi, i: (bi, 0, 0, 0, 0),
                         pipeline_mode=resident),
            _const_spec(gb.shape),
        ],
        out_specs=pl.BlockSpec((1, tq, MLA_W), lambda bi, i: (bi, i, 0)),
        out_shape=jax.ShapeDtypeStruct((b, s, MLA_W), BF16),
        scratch_shapes=[pltpu.VMEM((2, tk, tq), F32),
                        pltpu.VMEM((N_HEADS, V_DIM + SUM_ROWS, tq), F32)],
        compiler_params=pltpu.CompilerParams(dimension_semantics=("parallel", "arbitrary"),
                                             vmem_limit_bytes=VMEM_LIMIT),
        name="mla_attn",
    )(qt, k, vt, gb)


def _post_kernel(xp_ref, xm_ref, xn_ref, yap_ref, yam_ref, yan_ref, ybp_ref, ybm_ref, ybn_ref, mod_ref,
                 wo_ref, g2_ref, wup_ref, cw_ref, wdn_ref, gf_ref,
                 o_ref,
                 h_buf, a_buf, acc_buf, *, ts):
    i = pl.program_id(1)
    n_i = pl.num_programs(1)
    g2 = g2_ref[...]
    wo = wo_ref[...]
    gate1 = mod_ref[0, 2:3, :]
    shift2 = mod_ref[0, 3:4, :]
    scale2 = 1.0 + mod_ref[0, 4:5, :]
    gate2 = mod_ref[0, 5:6, :]

    def mix_residual(x, ya, yb):
        cat = jnp.concatenate([ya, yb], axis=-1)
        return x + gate1 * _dot(cat, wo)

    def norm_mod(x1):
        return _rms(x1, g2) * scale2 + shift2

    hp = norm_mod(mix_residual(xp_ref[0], yap_ref[0], ybp_ref[0]))
    h_buf[0:HALO, :] = jnp.where(i > 0, hp, 0.0).astype(BF16)
    x1 = mix_residual(xm_ref[0], yam_ref[0], ybm_ref[0])
    h_buf[HALO:HALO + ts, :] = norm_mod(x1).astype(BF16)
    hn = norm_mod(mix_residual(xn_ref[0], yan_ref[0], ybn_ref[0]))
    h_buf[HALO + ts:, :] = jnp.where(i < n_i - 1, hn, 0.0).astype(BF16)

    acc_buf[...] = jnp.zeros_like(acc_buf)

    def up_proj(c, slot):
        a_buf[slot] = _dot(h_buf[...], wup_ref[c])

    def gate_down(c, slot):
        cw = cw_ref[c]
        u = (a_buf[slot, HALO - 1:HALO - 1 + ts, :] * cw[0:1]
             + a_buf[slot, HALO:HALO + ts, :] * cw[1:2]
             + a_buf[slot, HALO + 1:HALO + 1 + ts, :] * cw[2:3])
        g = u[:, :FF_CHUNK]
        act = g / (1.0 + jnp.exp(-g)) * u[:, FF_CHUNK:]
        acc_buf[...] += _dot(act.astype(BF16), wdn_ref[c])

    def pair(cc, carry):
        c = 2 * cc
        up_proj(c + 1, 1)
        gate_down(c, 0)
        up_proj(c + 2, 0)
        gate_down(c + 1, 1)
        return carry

    assert N_FF_CHUNKS % 2 == 1
    up_proj(0, 0)
    lax.fori_loop(0, N_FF_CHUNKS // 2, pair, 0)
    gate_down(N_FF_CHUNKS - 1, 0)
    x2 = x1 + gate2 * acc_buf[...]
    o_ref[0] = _rms(x2, gf_ref[...])


def _post_call(x, ya, yb, mod, wts, *, ts):
    b, s, _ = x.shape
    n_i = s // ts
    hb = ts // HALO
    n_hb = s // HALO
    kern = functools.partial(_post_kernel, ts=ts)

    def three(c):
        return [
            pl.BlockSpec((1, HALO, c), lambda bi, i: (bi, jnp.maximum(i * hb - 1, 0), 0)),
            pl.BlockSpec((1, ts, c), lambda bi, i: (bi, i, 0)),
            pl.BlockSpec((1, HALO, c), lambda bi, i: (bi, jnp.minimum((i + 1) * hb, n_hb - 1), 0)),
        ]

    w_list = [wts["wo"], wts["g2"], wts["wup"], wts["conv_f"], wts["wdn"], wts["gf"]]
    return pl.pallas_call(
        kern,
        grid=(b, n_i),
        in_specs=three(D_MODEL) + three(A_W) + three(MLA_W)
        + [pl.BlockSpec((1, 6, D_MODEL), lambda bi, i: (bi, 0, 0))]
        + [_const_spec(w.shape) for w in w_list],
        out_specs=pl.BlockSpec((1, ts, D_MODEL), lambda bi, i: (bi, i, 0)),
        out_shape=jax.ShapeDtypeStruct((b, s, D_MODEL), F32),
        scratch_shapes=[pltpu.VMEM((ts + 2 * HALO, D_MODEL), BF16),
                        pltpu.VMEM((2, ts + 2 * HALO, 2 * FF_CHUNK), F32),
                        pltpu.VMEM((ts, D_MODEL), F32)],
        compiler_params=pltpu.CompilerParams(dimension_semantics=("parallel", "arbitrary"),
                                             vmem_limit_bytes=VMEM_LIMIT),
        name="post_mlp",
    )(x, x, x, ya, ya, ya, yb, yb, yb, mod, *w_list)


def _rot_half(w, axis):
    x1, x2 = jnp.split(w, 2, axis=axis)
    return jnp.concatenate([-x2, x1], axis=axis)


def _prep_weights(norm1_g, w_in, conv_a_w, q_norm_g, w_uq, kv_norm_g, w_ukv, out_norm_a_g, out_norm_b_g,
                  w_o, norm2_g, w_up, ffn_conv_w, w_down, final_g):
    cuts = np.cumsum([A_W, A_W, A_W, Q_LORA, KV_LORA]).tolist()
    w_h, w_b, w_c, w_cq, w_ckv, w_kr = jnp.split(w_in, cuts, axis=1)
    w1 = jnp.concatenate([w_h, w_c], axis=1)
    w2 = jnp.concatenate([w_b, w_cq, w_ckv, w_kr, _rot_half(w_kr, 1)], axis=1)

    wq3 = w_uq.reshape(Q_LORA, N_HEADS, QK_NOPE + QK_ROPE)
    wq3 = jnp.concatenate([wq3, _rot_half(wq3[:, :, QK_NOPE:], 2)], axis=2)
    wqt = wq3.reshape(Q_LORA, N_HEADS * HEAD_W).T

    wkv3 = w_ukv.reshape(KV_LORA, N_HEADS, QK_NOPE + V_DIM)
    wk = wkv3[:, :, :QK_NOPE].reshape(KV_LORA, N_HEADS * QK_NOPE)
    wvt = wkv3[:, :, QK_NOPE:].reshape(KV_LORA, N_HEADS * V_DIM).T

    wup = w_up.reshape(D_MODEL, 2, N_FF_CHUNKS, FF_CHUNK).transpose(2, 0, 1, 3).reshape(
        N_FF_CHUNKS, D_MODEL, 2 * FF_CHUNK)
    conv_f = ffn_conv_w.reshape(3, 2, N_FF_CHUNKS, FF_CHUNK).transpose(2, 0, 1, 3).reshape(
        N_FF_CHUNKS, 3, 2 * FF_CHUNK)
    wdn = w_down.reshape(N_FF_CHUNKS, FF_CHUNK, D_MODEL)
    row = lambda g: g.reshape(1, -1)
    return {
        "g1": row(norm1_g), "w1": w1.astype(BF16), "w2": w2.astype(BF16), "conv_a": conv_a_w,
        "gq": row(q_norm_g), "wqt": wqt.astype(BF16), "gkv": row(kv_norm_g), "wk": wk.astype(BF16),
        "wvt": wvt.astype(BF16), "ga": row(out_norm_a_g), "gb": row(out_norm_b_g), "wo": w_o.astype(BF16),
        "g2": row(norm2_g), "wup": wup.astype(BF16), "conv_f": conv_f, "wdn": wdn.astype(BF16),
        "gf": row(final_g),
    }


def _rope_table(s):
    inv = 1.0 / (ROPE_THETA ** (jnp.arange(0, QK_ROPE, 2, dtype=F32) / QK_ROPE))
    ang = jnp.arange(s, dtype=F32)[:, None] * inv[None, :]
    cos, sin = jnp.cos(ang), jnp.sin(ang)
    return jnp.concatenate([cos, cos, sin, sin], axis=1)


def _tiles(s):
    return min(512, s), min(512, s)


def _forward(x, mod, wts):
    s = x.shape[1]
    ts, tq = _tiles(s)
    ya, qt, k, vt = _pre_call(x, mod, _rope_table(s), wts, ts=ts)
    yb = _attn_call(qt, k, vt, wts["gb"], tq=tq)
    return _post_call(x, ya, yb, mod, wts, ts=ts)


def kernel(x_prompt, x_sample, c_prompt, c_sample, w_ada, b_ada, norm1_g, w_in, conv_a_w, q_norm_g, w_uq,
           kv_norm_g, w_ukv, out_norm_a_g, out_norm_b_g, w_o, norm2_g, w_up, ffn_conv_w, w_down, final_g):
    assert w_ada.shape[0] == 1, "single-layer kernel"
    wts = _prep_weights(norm1_g[0], w_in[0], conv_a_w[0], q_norm_g[0], w_uq[0], kv_norm_g[0], w_ukv[0],
                        out_norm_a_g[0], out_norm_b_g[0], w_o[0], norm2_g[0], w_up[0], ffn_conv_w[0],
                        w_down[0], final_g)
    nbp = c_prompt.shape[0]
    c_all = jnp.concatenate([c_prompt, c_sample], axis=0)
    mod = _mod_call(c_all, w_ada[0].astype(BF16), b_ada[0].reshape(1, -1))
    mod = mod.reshape(c_all.shape[0], 6, D_MODEL)
    y_prompt = _forward(x_prompt, mod[:nbp], wts)
    y_sample = _forward(x_sample, mod[nbp:], wts)
    return (y_prompt, y_sample)
```

```python
import functools
import math

import jax
import jax.numpy as jnp
import numpy as np
from jax import lax
from jax.experimental import pallas as pl
from jax.experimental.pallas import tpu as pltpu

D_MODEL = 1024
A_W = 512
N_HEADS = 4
QK_NOPE = 128
QK_ROPE = 64
V_DIM = 128
Q_LORA = 384
KV_LORA = 256
MLA_W = N_HEADS * V_DIM
D_FF = 2816
ROPE_THETA = 10000.0
ATTN_SCALE = 1.0 / math.sqrt(QK_NOPE + QK_ROPE)
Q_PRESCALE = ATTN_SCALE * math.log2(math.e)
EPS = 1e-6

LANES = 128
SUB = 8
MIX_W = A_W + MLA_W
HEAD_W = 2 * LANES
HALO = 16
SUM_ROWS = 16
ATTN_UNROLL = 8
FF_CHUNK = 256
N_FF_CHUNKS = D_FF // FF_CHUNK
VMEM_LIMIT = 56 * 1024 * 1024

BF16 = jnp.bfloat16
F32 = jnp.float32


def _rms(x, g):
    return x * lax.rsqrt(jnp.mean(x * x, axis=-1, keepdims=True) + EPS) * g


def _dot(a, b):
    return jnp.dot(a, b, preferred_element_type=F32)


def _dot_nt(a, b):
    return lax.dot_general(a, b, (((1,), (1,)), ((), ())), preferred_element_type=F32)


def _const_spec(shape):
    nd = len(shape)
    return pl.BlockSpec(shape, lambda *_: (0,) * nd, pipeline_mode=pl.Buffered(1))


def _mod_kernel(c_ref, w_ref, b_ref, o_ref):
    c = c_ref[...]
    act = c / (1.0 + jnp.exp(-c))
    o_ref[...] = _dot(act.astype(BF16), w_ref[...]) + b_ref[...]


def _mod_call(c_all, w_ada, b_ada):
    nb = c_all.shape[0]
    n_out = w_ada.shape[1]
    bn = D_MODEL
    return pl.pallas_call(
        _mod_kernel,
        grid=(n_out // bn,),
        in_specs=[
            pl.BlockSpec((nb, D_MODEL), lambda j: (0, 0)),
            pl.BlockSpec((D_MODEL, bn), lambda j: (0, j)),
            pl.BlockSpec((1, bn), lambda j: (0, j)),
        ],
        out_specs=pl.BlockSpec((nb, bn), lambda j: (0, j)),
        out_shape=jax.ShapeDtypeStruct((nb, n_out), F32),
        name="adaln_mod",
    )(c_all, w_ada, b_ada)


def _pre_kernel(xp_ref, xm_ref, xn_ref, mod_ref, cs_ref, cst_ref, g1_ref, w1_ref, w2_ref, cw_ref,
                gq_ref, wqt_ref, gkv_ref, wk_ref, wvt_ref, ga_ref,
                ya_ref, qt_ref, k_ref, vt_ref,
                h_buf, p_buf, *, ts):
    i = pl.program_id(1)
    n_i = pl.num_programs(1)
    g1 = g1_ref[...]
    scale = 1.0 + mod_ref[0, 1:2, :]
    shift = mod_ref[0, 0:1, :]

    def norm_mod(x):
        return (_rms(x, g1) * scale + shift).astype(BF16)

    h_buf[0:HALO, :] = norm_mod(xp_ref[0])
    h_buf[HALO:HALO + ts, :] = norm_mod(xm_ref[0])
    h_buf[HALO + ts:, :] = norm_mod(xn_ref[0])

    z1 = _dot(h_buf[...], w1_ref[...])
    p = z1[:, :A_W] * z1[:, A_W:]
    p_buf[...] = p
    p_buf[0:HALO, :] = jnp.where(i > 0, p[0:HALO], 0.0)
    p_buf[HALO + ts:, :] = jnp.where(i < n_i - 1, p[HALO + ts:], 0.0)
    cw = cw_ref[...]
    conv = (p_buf[HALO - 1:HALO - 1 + ts, :] * cw[0:1]
            + p_buf[HALO:HALO + ts, :] * cw[1:2]
            + p_buf[HALO + 1:HALO + 1 + ts, :] * cw[2:3])

    z2 = _dot(h_buf[HALO:HALO + ts, :], w2_ref[...])
    y_a = z2[:, :A_W] * conv
    ya_ref[0] = _rms(y_a, ga_ref[...]).astype(BF16)

    cs = cs_ref[...]
    keep = lax.broadcasted_iota(jnp.int32, (1, LANES), 1) < QK_ROPE
    o = A_W
    c_q = z2[:, o:o + Q_LORA]
    o += Q_LORA
    c_kv = z2[:, o:o + KV_LORA]
    o += KV_LORA
    t = z2[:, o:o + LANES] * cs
    k_rope = jnp.where(keep, t + pltpu.roll(t, QK_ROPE, axis=1), 0.0).astype(BF16)

    cqn = _rms(c_q, gq_ref[...]).astype(BF16)
    ckvn = _rms(c_kv, gkv_ref[...]).astype(BF16)
    qt = _dot_nt(wqt_ref[...], cqn)
    k_nope = _dot(ckvn, wk_ref[...])
    vt = _dot_nt(wvt_ref[...], ckvn)
    cos_t = cst_ref[0:QK_ROPE, :]
    sin_t = cst_ref[QK_ROPE:, :]
    for h in range(N_HEADS):
        b0 = h * HEAD_W
        r0 = b0 + QK_NOPE
        qt_ref[0, h, 0:QK_NOPE, :] = (qt[b0:r0] * Q_PRESCALE).astype(BF16)
        q_rope = qt[r0:r0 + QK_ROPE] * cos_t + qt[r0 + QK_ROPE:b0 + HEAD_W] * sin_t
        qt_ref[0, h, QK_NOPE:QK_NOPE + QK_ROPE, :] = (q_rope * Q_PRESCALE).astype(BF16)
        qt_ref[0, h, QK_NOPE + QK_ROPE:, :] = jnp.zeros((HEAD_W - QK_NOPE - QK_ROPE, ts), BF16)
        k_ref[0, h, :, 0:LANES] = k_nope[:, h * QK_NOPE:(h + 1) * QK_NOPE].astype(BF16)
        k_ref[0, h, :, LANES:] = k_rope
        vt_ref[0, h, 0] = vt[h * V_DIM:(h + 1) * V_DIM].astype(BF16)


def _pre_call(x, mod, cs, wts, *, ts):
    b, s, _ = x.shape
    n_i = s // ts
    hb = ts // HALO
    n_hb = s // HALO
    kern = functools.partial(_pre_kernel, ts=ts)
    w_list = [wts["g1"], wts["w1"], wts["w2"], wts["conv_a"], wts["gq"], wts["wqt"], wts["gkv"], wts["wk"],
              wts["wvt"], wts["ga"]]
    return pl.pallas_call(
        kern,
        grid=(b, n_i),
        in_specs=[
            pl.BlockSpec((1, HALO, D_MODEL), lambda bi, i: (bi, jnp.maximum(i * hb - 1, 0), 0)),
            pl.BlockSpec((1, ts, D_MODEL), lambda bi, i: (bi, i, 0)),
            pl.BlockSpec((1, HALO, D_MODEL), lambda bi, i: (bi, jnp.minimum((i + 1) * hb, n_hb - 1), 0)),
            pl.BlockSpec((1, 6, D_MODEL), lambda bi, i: (bi, 0, 0)),
            pl.BlockSpec((ts, LANES), lambda bi, i: (i, 0)),
            pl.BlockSpec((LANES, ts), lambda bi, i: (0, i)),
        ] + [_const_spec(w.shape) for w in w_list],
        out_specs=[
            pl.BlockSpec((1, ts, A_W), lambda bi, i: (bi, i, 0)),
            pl.BlockSpec((1, N_HEADS, HEAD_W, ts), lambda bi, i: (bi, 0, 0, i)),
            pl.BlockSpec((1, N_HEADS, ts, HEAD_W), lambda bi, i: (bi, 0, i, 0)),
            pl.BlockSpec((1, N_HEADS, 1, V_DIM, ts), lambda bi, i: (bi, 0, i, 0, 0)),
        ],
        out_shape=[
            jax.ShapeDtypeStruct((b, s, A_W), BF16),
            jax.ShapeDtypeStruct((b, N_HEADS, HEAD_W, s), BF16),
            jax.ShapeDtypeStruct((b, N_HEADS, s, HEAD_W), BF16),
            jax.ShapeDtypeStruct((b, N_HEADS, n_i, V_DIM, ts), BF16),
        ],
        scratch_shapes=[pltpu.VMEM((ts + 2 * HALO, D_MODEL), BF16),
                        pltpu.VMEM((ts + 2 * HALO, A_W), F32)],
        compiler_params=pltpu.CompilerParams(dimension_semantics=("parallel", "arbitrary"),
                                             vmem_limit_bytes=VMEM_LIMIT),
        name="pre_attn",
    )(x, x, x, mod, cs, cs.T, *w_list)


def _attn_kernel(qt_ref, k_ref, vt_ref, gb_ref, o_ref, s_buf, acc_buf):
    n_chunks, _, tk = vt_ref.shape[2:]
    tq = qt_ref.shape[3]
    n_items = N_HEADS * n_chunks
    unroll = ATTN_UNROLL if n_items % ATTN_UNROLL == 0 else N_HEADS
    ones = jnp.ones((SUM_ROWS, tk), BF16)

    def scores(n, slot):
        h = lax.div(n, n_chunks)
        k0 = pl.multiple_of(lax.rem(n, n_chunks) * tk, tk)
        st = _dot(k_ref[0, h, pl.ds(k0, tk), :], qt_ref[0, h])
        s_buf[slot] = st
        return jnp.max(st, axis=0, keepdims=True)

    def accumulate(n, slot, cmax, m, acc):
        h = lax.div(n, n_chunks)
        j = lax.rem(n, n_chunks)
        m = jnp.where(j == 0, -jnp.inf, m)
        m_new = jnp.maximum(m, cmax)
        alpha = jnp.exp2(m - m_new)
        pt = jnp.exp2(s_buf[slot] - m_new).astype(BF16)
        v_ext = jnp.concatenate([vt_ref[0, h, j], ones], axis=0)
        acc = acc * alpha + _dot(v_ext, pt)
        acc_buf[h] = acc
        return m_new, acc

    def group(nn, carry):
        m, acc, cmax = carry
        for u in range(unroll):
            n = unroll * nn + u
            slot = u % 2
            nxt = n + 1 if u < unroll - 1 else jnp.minimum(n + 1, n_items - 1)
            cmax_next = scores(nxt, 1 - slot)
            m, acc = accumulate(n, slot, cmax, m, acc)
            cmax = cmax_next
        return m, acc, cmax

    init = (jnp.full((1, tq), -jnp.inf, F32), jnp.zeros((V_DIM + SUM_ROWS, tq), F32), scores(0, 0))
    lax.fori_loop(0, n_items // unroll, group, init)
    heads = []
    for h in range(N_HEADS):
        acc = acc_buf[h]
        heads.append((acc[:V_DIM] / acc[V_DIM:V_DIM + 1]).T)
    y_b = jnp.concatenate(heads, axis=1)
    o_ref[0] = _rms(y_b, gb_ref[...]).astype(BF16)


def _attn_call(qt, k, vt, gb, *, tq):
    b, _, _, s = qt.shape
    n_chunks, _, tk = vt.shape[2:]
    resident = pl.Buffered(1)
    return pl.pallas_call(
        _attn_kernel,
        grid=(b, s // tq),
        in_specs=[
            pl.BlockSpec((1, N_HEADS, HEAD_W, tq), lambda bi, i: (bi, 0, 0, i)),
            pl.BlockSpec((1, N_HEADS, s, HEAD_W), lambda bi, i: (bi, 0, 0, 0), pipeline_mode=resident),
            pl.BlockSpec((1, N_HEADS, n_chunks, V_DIM, tk), lambda bi, i: (bi, 0, 0, 0, 0),
                         pipeline_mode=resident),
            _const_spec(gb.shape),
        ],
        out_specs=pl.BlockSpec((1, tq, MLA_W), lambda bi, i: (bi, i, 0)),
        out_shape=jax.ShapeDtypeStruct((b, s, MLA_W), BF16),
        scratch_shapes=[pltpu.VMEM((2, tk, tq), F32),
                        pltpu.VMEM((N_HEADS, V_DIM + SUM_ROWS, tq), F32)],
        compiler_params=pltpu.CompilerParams(dimension_semantics=("parallel", "arbitrary"),
                                             vmem_limit_bytes=VMEM_LIMIT),
        name="mla_attn",
    )(qt, k, vt, gb)


def _post_kernel(xp_ref, xm_ref, xn_ref, yap_ref, yam_ref, yan_ref, ybp_ref, ybm_ref, ybn_ref, mod_ref,
                 wo_ref, g2_ref, wup_ref, cw_ref, wdn_ref, gf_ref,
                 o_ref,
                 cat_buf, n_buf, h_buf, a_buf, acc_buf, *, ts):
    i = pl.program_id(1)
    n_i = pl.num_programs(1)
    g2 = g2_ref[...]
    gate1 = mod_ref[0, 2:3, :]
    shift2 = mod_ref[0, 3:4, :]
    scale2 = 1.0 + mod_ref[0, 4:5, :]
    gate2 = mod_ref[0, 5:6, :]
    rows = ts + 2 * SUB
    seg = rows // SUB
    n_slab = D_MODEL // LANES

    cat_buf[0:HALO, 0:A_W] = yap_ref[0]
    cat_buf[0:HALO, A_W:] = ybp_ref[0]
    cat_buf[HALO:HALO + ts, 0:A_W] = yam_ref[0]
    cat_buf[HALO:HALO + ts, A_W:] = ybm_ref[0]
    cat_buf[HALO + ts:, 0:A_W] = yan_ref[0]
    cat_buf[HALO + ts:, A_W:] = ybn_ref[0]
    y = _dot(cat_buf[...], wo_ref[...])

    def norm_mod(x1):
        return _rms(x1, g2) * scale2 + shift2

    x1 = xm_ref[0] + gate1 * y[HALO:HALO + ts]
    h_prev = norm_mod(xp_ref[0, HALO - SUB:, :] + gate1 * y[HALO - SUB:HALO])
    h_prev = jnp.where(i > 0, h_prev, 0.0)
    h_main = norm_mod(x1)
    h_next = norm_mod(xn_ref[0, :SUB, :] + gate1 * y[HALO + ts:HALO + ts + SUB])
    h_next = jnp.where(i < n_i - 1, h_next, 0.0)

    for c in range(n_slab):
        lanes = slice(c * LANES, (c + 1) * LANES)
        n_buf[c, 0:SUB, :] = h_prev[:, lanes]
        n_buf[c, SUB:SUB + ts, :] = h_main[:, lanes]
        n_buf[c, SUB + ts:, :] = h_next[:, lanes]
    for kk in range(seg // 2):
        blocks = [jnp.concatenate([n_buf[c, pl.ds(k, SUB, stride=seg), :] for c in range(n_slab)], axis=1)
                  for k in (2 * kk, 2 * kk + 1)]
        h_buf[2 * SUB * kk:2 * SUB * (kk + 1), :] = jnp.concatenate(blocks, axis=0).astype(BF16)

    acc_buf[...] = jnp.zeros_like(acc_buf)

    def up_proj(c, slot):
        a_buf[slot] = _dot(h_buf[...], wup_ref[c])

    def gate_down(c, slot):
        cw = cw_ref[c]
        w0, w1, w2 = cw[0:1], cw[1:2], cw[2:3]
        a = a_buf.at[slot]
        u_first = (pltpu.roll(a[rows - SUB:rows, :], 1, axis=0) * w0 + a[0:SUB, :] * w1 + a[SUB:2 * SUB, :] * w2)
        u_mid = a[0:rows - 2 * SUB, :] * w0 + a[SUB:rows - SUB, :] * w1 + a[2 * SUB:rows, :] * w2
        u_last = (a[rows - 2 * SUB:rows - SUB, :] * w0 + a[rows - SUB:rows, :] * w1
                  + pltpu.roll(a[0:SUB, :], SUB - 1, axis=0) * w2)
        u = jnp.concatenate([u_first, u_mid, u_last], axis=0)
        g = u[:, :FF_CHUNK]
        act = g / (1.0 + jnp.exp(-g)) * u[:, FF_CHUNK:]
        acc_buf[...] += _dot(act.astype(BF16), wdn_ref[c])

    def pair(cc, carry):
        c = 2 * cc
        up_proj(c + 1, 1)
        gate_down(c, 0)
        up_proj(c + 2, 0)
        gate_down(c + 1, 1)
        return carry

    assert N_FF_CHUNKS % 2 == 1
    up_proj(0, 0)
    lax.fori_loop(0, N_FF_CHUNKS // 2, pair, 0)
    gate_down(N_FF_CHUNKS - 1, 0)

    for k in range(seg):
        for c in range(n_slab):
            n_buf[c, pl.ds(k, SUB, stride=seg), :] = acc_buf[SUB * k:SUB * (k + 1), c * LANES:(c + 1) * LANES]
    mlp = jnp.concatenate([n_buf[c, SUB:SUB + ts, :] for c in range(n_slab)], axis=1)
    x2 = x1 + gate2 * mlp
    o_ref[0] = _rms(x2, gf_ref[...])


def _post_call(x, ya, yb, mod, wts, *, ts):
    b, s, _ = x.shape
    n_i = s // ts
    hb = ts // HALO
    n_hb = s // HALO
    kern = functools.partial(_post_kernel, ts=ts)

    def three(c):
        return [
            pl.BlockSpec((1, HALO, c), lambda bi, i: (bi, jnp.maximum(i * hb - 1, 0), 0)),
            pl.BlockSpec((1, ts, c), lambda bi, i: (bi, i, 0)),
            pl.BlockSpec((1, HALO, c), lambda bi, i: (bi, jnp.minimum((i + 1) * hb, n_hb - 1), 0)),
        ]

    rows = ts + 2 * SUB
    assert rows % (2 * SUB) == 0 and (rows // SUB) % 8 != 0
    w_list = [wts["wo"], wts["g2"], wts["wup"], wts["conv_f"], wts["wdn"], wts["gf"]]
    return pl.pallas_call(
        kern,
        grid=(b, n_i),
        in_specs=three(D_MODEL) + three(A_W) + three(MLA_W)
        + [pl.BlockSpec((1, 6, D_MODEL), lambda bi, i: (bi, 0, 0))]
        + [_const_spec(w.shape) for w in w_list],
        out_specs=pl.BlockSpec((1, ts, D_MODEL), lambda bi, i: (bi, i, 0)),
        out_shape=jax.ShapeDtypeStruct((b, s, D_MODEL), F32),
        scratch_shapes=[pltpu.VMEM((ts + 2 * HALO, MIX_W), BF16),
                        pltpu.VMEM((D_MODEL // LANES, rows, LANES), F32),
                        pltpu.VMEM((rows, D_MODEL), BF16),
                        pltpu.VMEM((2, rows, 2 * FF_CHUNK), F32),
                        pltpu.VMEM((rows, D_MODEL), F32)],
        compiler_params=pltpu.CompilerParams(dimension_semantics=("parallel", "arbitrary"),
                                             vmem_limit_bytes=VMEM_LIMIT),
        name="post_mlp",
    )(x, x, x, ya, ya, ya, yb, yb, yb, mod, *w_list)


def _rot_half(w, axis):
    x1, x2 = jnp.split(w, 2, axis=axis)
    return jnp.concatenate([-x2, x1], axis=axis)


def _prep_weights(norm1_g, w_in, conv_a_w, q_norm_g, w_uq, kv_norm_g, w_ukv, out_norm_a_g, out_norm_b_g,
                  w_o, norm2_g, w_up, ffn_conv_w, w_down, final_g):
    cuts = np.cumsum([A_W, A_W, A_W, Q_LORA, KV_LORA]).tolist()
    w_h, w_b, w_c, w_cq, w_ckv, w_kr = jnp.split(w_in, cuts, axis=1)
    w1 = jnp.concatenate([w_h, w_c], axis=1)
    w2 = jnp.concatenate([w_b, w_cq, w_ckv, w_kr, _rot_half(w_kr, 1)], axis=1)

    wq3 = w_uq.reshape(Q_LORA, N_HEADS, QK_NOPE + QK_ROPE)
    wq3 = jnp.concatenate([wq3, _rot_half(wq3[:, :, QK_NOPE:], 2)], axis=2)
    wqt = wq3.reshape(Q_LORA, N_HEADS * HEAD_W).T

    wkv3 = w_ukv.reshape(KV_LORA, N_HEADS, QK_NOPE + V_DIM)
    wk = wkv3[:, :, :QK_NOPE].reshape(KV_LORA, N_HEADS * QK_NOPE)
    wvt = wkv3[:, :, QK_NOPE:].reshape(KV_LORA, N_HEADS * V_DIM).T

    wup = w_up.reshape(D_MODEL, 2, N_FF_CHUNKS, FF_CHUNK).transpose(2, 0, 1, 3).reshape(
        N_FF_CHUNKS, D_MODEL, 2 * FF_CHUNK)
    conv_f = ffn_conv_w.reshape(3, 2, N_FF_CHUNKS, FF_CHUNK).transpose(2, 0, 1, 3).reshape(
        N_FF_CHUNKS, 3, 2 * FF_CHUNK)
    wdn = w_down.reshape(N_FF_CHUNKS, FF_CHUNK, D_MODEL)
    row = lambda g: g.reshape(1, -1)
    return {
        "g1": row(norm1_g), "w1": w1.astype(BF16), "w2": w2.astype(BF16), "conv_a": conv_a_w,
        "gq": row(q_norm_g), "wqt": wqt.astype(BF16), "gkv": row(kv_norm_g), "wk": wk.astype(BF16),
        "wvt": wvt.astype(BF16), "ga": row(out_norm_a_g), "gb": row(out_norm_b_g), "wo": w_o.astype(BF16),
        "g2": row(norm2_g), "wup": wup.astype(BF16), "conv_f": conv_f, "wdn": wdn.astype(BF16),
        "gf": row(final_g),
    }


def _rope_table(s):
    inv = 1.0 / (ROPE_THETA ** (jnp.arange(0, QK_ROPE, 2, dtype=F32) / QK_ROPE))
    ang = jnp.arange(s, dtype=F32)[:, None] * inv[None, :]
    cos, sin = jnp.cos(ang), jnp.sin(ang)
    return jnp.concatenate([cos, cos, sin, sin], axis=1)


def _tiles(s):
    return min(512, s), min(512, s)


def _forward(x, mod, wts):
    s = x.shape[1]
    ts, tq = _tiles(s)
    ya, qt, k, vt = _pre_call(x, mod, _rope_table(s), wts, ts=ts)
    yb = _attn_call(qt, k, vt, wts["gb"], tq=tq)
    return _post_call(x, ya, yb, mod, wts, ts=ts)


def kernel(x_prompt, x_sample, c_prompt, c_sample, w_ada, b_ada, norm1_g, w_in, conv_a_w, q_norm_g, w_uq,
           kv_norm_g, w_ukv, out_norm_a_g, out_norm_b_g, w_o, norm2_g, w_up, ffn_conv_w, w_down, final_g):
    assert w_ada.shape[0] == 1, "single-layer kernel"
    wts = _prep_weights(norm1_g[0], w_in[0], conv_a_w[0], q_norm_g[0], w_uq[0], kv_norm_g[0], w_ukv[0],
                        out_norm_a_g[0], out_norm_b_g[0], w_o[0], norm2_g[0], w_up[0], ffn_conv_w[0],
                        w_down[0], final_g)
    nbp = c_prompt.shape[0]
    c_all = jnp.concatenate([c_prompt, c_sample], axis=0)
    mod = _mod_call(c_all, w_ada[0].astype(BF16), b_ada[0].reshape(1, -1))
    mod = mod.reshape(c_all.shape[0], 6, D_MODEL)
    y_prompt = _forward(x_prompt, mod[:nbp], wts)
    y_sample = _forward(x_sample, mod[nbp:], wts)
    return (y_prompt, y_sample)
```

```python
import functools
import math

import jax
import jax.numpy as jnp
import numpy as np
from jax import lax
from jax.experimental import pallas as pl
from jax.experimental.pallas import tpu as pltpu

D_MODEL = 1024
A_W = 512
N_HEADS = 4
QK_NOPE = 128
QK_ROPE = 64
V_DIM = 128
Q_LORA = 384
KV_LORA = 256
MLA_W = N_HEADS * V_DIM
D_FF = 2816
ROPE_THETA = 10000.0
ATTN_SCALE = 1.0 / math.sqrt(QK_NOPE + QK_ROPE)
Q_PRESCALE = ATTN_SCALE * math.log2(math.e)
EPS = 1e-6

LANES = 128
SUB = 8
MIX_W = A_W + MLA_W
HEAD_W = 2 * LANES
HALO = 16
SUM_ROWS = 16
ATTN_UNROLL = 16
FF_CHUNK = 256
N_FF_CHUNKS = D_FF // FF_CHUNK
VMEM_LIMIT = 56 * 1024 * 1024

BF16 = jnp.bfloat16
F32 = jnp.float32


def _rms(x, g):
    return x * lax.rsqrt(jnp.mean(x * x, axis=-1, keepdims=True) + EPS) * g


def _dot(a, b):
    return jnp.dot(a, b, preferred_element_type=F32)


def _dot_nt(a, b):
    return lax.dot_general(a, b, (((1,), (1,)), ((), ())), preferred_element_type=F32)


def _const_spec(shape):
    nd = len(shape)
    return pl.BlockSpec(shape, lambda *_: (0,) * nd, pipeline_mode=pl.Buffered(1))


def _mod_kernel(c_ref, w_ref, b_ref, o_ref):
    c = c_ref[...]
    act = c / (1.0 + jnp.exp(-c))
    o_ref[...] = _dot(act.astype(BF16), w_ref[...]) + b_ref[...]


def _mod_call(c_all, w_ada, b_ada):
    nb = c_all.shape[0]
    n_out = w_ada.shape[1]
    bn = D_MODEL
    return pl.pallas_call(
        _mod_kernel,
        grid=(n_out // bn,),
        in_specs=[
            pl.BlockSpec((nb, D_MODEL), lambda j: (0, 0)),
            pl.BlockSpec((D_MODEL, bn), lambda j: (0, j)),
            pl.BlockSpec((1, bn), lambda j: (0, j)),
        ],
        out_specs=pl.BlockSpec((nb, bn), lambda j: (0, j)),
        out_shape=jax.ShapeDtypeStruct((nb, n_out), F32),
        name="adaln_mod",
    )(c_all, w_ada, b_ada)


def _pre_kernel(xp_ref, xm_ref, xn_ref, mod_ref, cs_ref, cst_ref, g1_ref, w1_ref, w2_ref, cw_ref,
                gq_ref, wqt_ref, gkv_ref, wk_ref, wvt_ref, ga_ref,
                ya_ref, qt_ref, k_ref, kr_ref, vt_ref,
                h_buf, p_buf, *, ts):
    i = pl.program_id(1)
    n_i = pl.num_programs(1)
    g1 = g1_ref[...]
    scale = 1.0 + mod_ref[0, 1:2, :]
    shift = mod_ref[0, 0:1, :]

    def norm_mod(x):
        return (_rms(x, g1) * scale + shift).astype(BF16)

    h_buf[0:HALO, :] = norm_mod(xp_ref[0])
    h_buf[HALO:HALO + ts, :] = norm_mod(xm_ref[0])
    h_buf[HALO + ts:, :] = norm_mod(xn_ref[0])

    z1 = _dot(h_buf[...], w1_ref[...])
    p = z1[:, :A_W] * z1[:, A_W:]
    p_buf[...] = p
    p_buf[0:HALO, :] = jnp.where(i > 0, p[0:HALO], 0.0)
    p_buf[HALO + ts:, :] = jnp.where(i < n_i - 1, p[HALO + ts:], 0.0)
    cw = cw_ref[...]
    conv = (p_buf[HALO - 1:HALO - 1 + ts, :] * cw[0:1]
            + p_buf[HALO:HALO + ts, :] * cw[1:2]
            + p_buf[HALO + 1:HALO + 1 + ts, :] * cw[2:3])

    z2 = _dot(h_buf[HALO:HALO + ts, :], w2_ref[...])
    y_a = z2[:, :A_W] * conv
    ya_ref[0] = _rms(y_a, ga_ref[...]).astype(BF16)

    cs = cs_ref[...]
    keep = lax.broadcasted_iota(jnp.int32, (1, LANES), 1) < QK_ROPE
    o = A_W
    c_q = z2[:, o:o + Q_LORA]
    o += Q_LORA
    c_kv = z2[:, o:o + KV_LORA]
    o += KV_LORA
    t = z2[:, o:o + LANES] * cs
    kr_ref[0] = jnp.where(keep, t + pltpu.roll(t, QK_ROPE, axis=1), 0.0).astype(BF16)

    cqn = _rms(c_q, gq_ref[...]).astype(BF16)
    ckvn = _rms(c_kv, gkv_ref[...]).astype(BF16)
    qt = _dot_nt(wqt_ref[...], cqn)
    k_nope = _dot(ckvn, wk_ref[...])
    vt = _dot_nt(wvt_ref[...], ckvn)
    cos_t = cst_ref[0:QK_ROPE, :]
    sin_t = cst_ref[QK_ROPE:, :]
    for h in range(N_HEADS):
        b0 = h * HEAD_W
        r0 = b0 + QK_NOPE
        qt_ref[0, h, 0:QK_NOPE, :] = (qt[b0:r0] * Q_PRESCALE).astype(BF16)
        q_rope = qt[r0:r0 + QK_ROPE] * cos_t + qt[r0 + QK_ROPE:b0 + HEAD_W] * sin_t
        qt_ref[0, h, QK_NOPE:QK_NOPE + QK_ROPE, :] = (q_rope * Q_PRESCALE).astype(BF16)
        qt_ref[0, h, QK_NOPE + QK_ROPE:, :] = jnp.zeros((HEAD_W - QK_NOPE - QK_ROPE, ts), BF16)
        k_ref[0, h] = k_nope[:, h * QK_NOPE:(h + 1) * QK_NOPE].astype(BF16)
        vt_ref[0, h, 0] = vt[h * V_DIM:(h + 1) * V_DIM].astype(BF16)


def _pre_call(x, mod, cs, wts, *, ts):
    b, s, _ = x.shape
    n_i = s // ts
    hb = ts // HALO
    n_hb = s // HALO
    kern = functools.partial(_pre_kernel, ts=ts)
    w_list = [wts["g1"], wts["w1"], wts["w2"], wts["conv_a"], wts["gq"], wts["wqt"], wts["gkv"], wts["wk"],
              wts["wvt"], wts["ga"]]
    return pl.pallas_call(
        kern,
        grid=(b, n_i),
        in_specs=[
            pl.BlockSpec((1, HALO, D_MODEL), lambda bi, i: (bi, jnp.maximum(i * hb - 1, 0), 0)),
            pl.BlockSpec((1, ts, D_MODEL), lambda bi, i: (bi, i, 0)),
            pl.BlockSpec((1, HALO, D_MODEL), lambda bi, i: (bi, jnp.minimum((i + 1) * hb, n_hb - 1), 0)),
            pl.BlockSpec((1, 6, D_MODEL), lambda bi, i: (bi, 0, 0)),
            pl.BlockSpec((ts, LANES), lambda bi, i: (i, 0)),
            pl.BlockSpec((LANES, ts), lambda bi, i: (0, i)),
        ] + [_const_spec(w.shape) for w in w_list],
        out_specs=[
            pl.BlockSpec((1, ts, A_W), lambda bi, i: (bi, i, 0)),
            pl.BlockSpec((1, N_HEADS, HEAD_W, ts), lambda bi, i: (bi, 0, 0, i)),
            pl.BlockSpec((1, N_HEADS, ts, QK_NOPE), lambda bi, i: (bi, 0, i, 0)),
            pl.BlockSpec((1, ts, LANES), lambda bi, i: (bi, i, 0)),
            pl.BlockSpec((1, N_HEADS, 1, V_DIM, ts), lambda bi, i: (bi, 0, i, 0, 0)),
        ],
        out_shape=[
            jax.ShapeDtypeStruct((b, s, A_W), BF16),
            jax.ShapeDtypeStruct((b, N_HEADS, HEAD_W, s), BF16),
            jax.ShapeDtypeStruct((b, N_HEADS, s, QK_NOPE), BF16),
            jax.ShapeDtypeStruct((b, s, LANES), BF16),
            jax.ShapeDtypeStruct((b, N_HEADS, n_i, V_DIM, ts), BF16),
        ],
        scratch_shapes=[pltpu.VMEM((ts + 2 * HALO, D_MODEL), BF16),
                        pltpu.VMEM((ts + 2 * HALO, A_W), F32)],
        compiler_params=pltpu.CompilerParams(dimension_semantics=("parallel", "arbitrary"),
                                             vmem_limit_bytes=VMEM_LIMIT),
        name="pre_attn",
    )(x, x, x, mod, cs, cs.T, *w_list)


def _attn_kernel(qt_ref, k_ref, kr_ref, vt_ref, gb_ref, o_ref, s_buf, acc_buf):
    n_chunks, _, tk = vt_ref.shape[2:]
    tq = qt_ref.shape[3]
    n_items = N_HEADS * n_chunks
    unroll = next(u for u in (ATTN_UNROLL, ATTN_UNROLL // 2, N_HEADS) if n_items % u == 0 and n_items >= 2 * u
                  or u == N_HEADS)
    ones = jnp.ones((SUM_ROWS, tk), BF16)

    def scores(n, slot):
        h = lax.div(n, n_chunks)
        k0 = pl.multiple_of(lax.rem(n, n_chunks) * tk, tk)
        keys = jnp.concatenate([k_ref[0, h, pl.ds(k0, tk), :], kr_ref[0, pl.ds(k0, tk), :]], axis=1)
        st = _dot(keys, qt_ref[0, h])
        s_buf[slot] = st
        return jnp.max(st, axis=0, keepdims=True)

    def accumulate(n, slot, cmax, m, acc):
        h = lax.div(n, n_chunks)
        j = lax.rem(n, n_chunks)
        m = jnp.where(j == 0, -jnp.inf, m)
        m_new = jnp.maximum(m, cmax)
        alpha = jnp.exp2(m - m_new)
        pt = jnp.exp2(s_buf[slot] - m_new).astype(BF16)
        v_ext = jnp.concatenate([vt_ref[0, h, j], ones], axis=0)
        acc = acc * alpha + _dot(v_ext, pt)
        acc_buf[h] = acc
        return m_new, acc

    def group(nn, carry):
        m, acc, cmax = carry
        for u in range(unroll):
            n = unroll * nn + u
            slot = u % 2
            nxt = n + 1 if u < unroll - 1 else jnp.minimum(n + 1, n_items - 1)
            cmax_next = scores(nxt, 1 - slot)
            m, acc = accumulate(n, slot, cmax, m, acc)
            cmax = cmax_next
        return m, acc, cmax

    init = (jnp.full((1, tq), -jnp.inf, F32), jnp.zeros((V_DIM + SUM_ROWS, tq), F32), scores(0, 0))
    lax.fori_loop(0, n_items // unroll, group, init)
    heads = []
    for h in range(N_HEADS):
        acc = acc_buf[h]
        heads.append((acc[:V_DIM] / acc[V_DIM:V_DIM + 1]).T)
    y_b = jnp.concatenate(heads, axis=1)
    o_ref[0] = _rms(y_b, gb_ref[...]).astype(BF16)


def _attn_call(qt, k, kr, vt, gb, *, tq):
    b, _, _, s = qt.shape
    n_chunks, _, tk = vt.shape[2:]
    return pl.pallas_call(
        _attn_kernel,
        grid=(b, s // tq),
        in_specs=[
            pl.BlockSpec((1, N_HEADS, HEAD_W, tq), lambda bi, i: (bi, 0, 0, i)),
            pl.BlockSpec((1, N_HEADS, s, QK_NOPE), lambda bi, i: (bi, 0, 0, 0)),
            pl.BlockSpec((1, s, LANES), lambda bi, i: (bi, 0, 0)),
            pl.BlockSpec((1, N_HEADS, n_chunks, V_DIM, tk), lambda bi, i: (bi, 0, 0, 0, 0)),
            _const_spec(gb.shape),
        ],
        out_specs=pl.BlockSpec((1, tq, MLA_W), lambda bi, i: (bi, i, 0)),
        out_shape=jax.ShapeDtypeStruct((b, s, MLA_W), BF16),
        scratch_shapes=[pltpu.VMEM((2, tk, tq), F32),
                        pltpu.VMEM((N_HEADS, V_DIM + SUM_ROWS, tq), F32)],
        compiler_params=pltpu.CompilerParams(dimension_semantics=("parallel", "arbitrary"),
                                             vmem_limit_bytes=VMEM_LIMIT),
        name="mla_attn",
    )(qt, k, kr, vt, gb)


def _post_kernel(xp_ref, xm_ref, xn_ref, yap_ref, yam_ref, yan_ref, ybp_ref, ybm_ref, ybn_ref, mod_ref,
                 wo_ref, g2_ref, wup_ref, cw_ref, wdn_ref, gf_ref,
                 o_ref,
                 cat_buf, n_buf, h_buf, a_buf, acc_buf, *, ts):
    i = pl.program_id(1)
    n_i = pl.num_programs(1)
    g2 = g2_ref[...]
    gate1 = mod_ref[0, 2:3, :]
    shift2 = mod_ref[0, 3:4, :]
    scale2 = 1.0 + mod_ref[0, 4:5, :]
    gate2 = mod_ref[0, 5:6, :]
    rows = ts + 2 * SUB
    seg = rows // SUB
    n_slab = D_MODEL // LANES

    cat_buf[0:HALO, 0:A_W] = yap_ref[0]
    cat_buf[0:HALO, A_W:] = ybp_ref[0]
    cat_buf[HALO:HALO + ts, 0:A_W] = yam_ref[0]
    cat_buf[HALO:HALO + ts, A_W:] = ybm_ref[0]
    cat_buf[HALO + ts:, 0:A_W] = yan_ref[0]
    cat_buf[HALO + ts:, A_W:] = ybn_ref[0]
    y = _dot(cat_buf[...], wo_ref[...])

    def norm_mod(x1):
        return _rms(x1, g2) * scale2 + shift2

    x1 = xm_ref[0] + gate1 * y[HALO:HALO + ts]
    h_prev = norm_mod(xp_ref[0, HALO - SUB:, :] + gate1 * y[HALO - SUB:HALO])
    h_prev = jnp.where(i > 0, h_prev, 0.0)
    h_main = norm_mod(x1)
    h_next = norm_mod(xn_ref[0, :SUB, :] + gate1 * y[HALO + ts:HALO + ts + SUB])
    h_next = jnp.where(i < n_i - 1, h_next, 0.0)

    for c in range(n_slab):
        lanes = slice(c * LANES, (c + 1) * LANES)
        n_buf[c, 0:SUB, :] = h_prev[:, lanes]
        n_buf[c, SUB:SUB + ts, :] = h_main[:, lanes]
        n_buf[c, SUB + ts:, :] = h_next[:, lanes]
    for kk in range(seg // 2):
        blocks = [jnp.concatenate([n_buf[c, pl.ds(k, SUB, stride=seg), :] for c in range(n_slab)], axis=1)
                  for k in (2 * kk, 2 * kk + 1)]
        h_buf[2 * SUB * kk:2 * SUB * (kk + 1), :] = jnp.concatenate(blocks, axis=0).astype(BF16)

    acc_buf[...] = jnp.zeros_like(acc_buf)

    def up_proj(c, slot):
        a_buf[slot] = _dot(h_buf[...], wup_ref[c])

    def gate_down(c, slot):
        cw = cw_ref[c]
        w0, w1, w2 = cw[0:1], cw[1:2], cw[2:3]
        a = a_buf.at[slot]
        u_first = (pltpu.roll(a[rows - SUB:rows, :], 1, axis=0) * w0 + a[0:SUB, :] * w1 + a[SUB:2 * SUB, :] * w2)
        u_mid = a[0:rows - 2 * SUB, :] * w0 + a[SUB:rows - SUB, :] * w1 + a[2 * SUB:rows, :] * w2
        u_last = (a[rows - 2 * SUB:rows - SUB, :] * w0 + a[rows - SUB:rows, :] * w1
                  + pltpu.roll(a[0:SUB, :], SUB - 1, axis=0) * w2)
        u = jnp.concatenate([u_first, u_mid, u_last], axis=0)
        g = u[:, :FF_CHUNK]
        act = g / (1.0 + jnp.exp(-g)) * u[:, FF_CHUNK:]
        acc_buf[...] += _dot(act.astype(BF16), wdn_ref[c])

    def pair(cc, carry):
        c = 2 * cc
        up_proj(c + 1, 1)
        gate_down(c, 0)
        up_proj(c + 2, 0)
        gate_down(c + 1, 1)
        return carry

    assert N_FF_CHUNKS % 2 == 1
    up_proj(0, 0)
    lax.fori_loop(0, N_FF_CHUNKS // 2, pair, 0)
    gate_down(N_FF_CHUNKS - 1, 0)

    for k in range(seg):
        for c in range(n_slab):
            n_buf[c, pl.ds(k, SUB, stride=seg), :] = acc_buf[SUB * k:SUB * (k + 1), c * LANES:(c + 1) * LANES]
    mlp = jnp.concatenate([n_buf[c, SUB:SUB + ts, :] for c in range(n_slab)], axis=1)
    x2 = x1 + gate2 * mlp
    o_ref[0] = _rms(x2, gf_ref[...])


def _post_call(x, ya, yb, mod, wts, *, ts):
    b, s, _ = x.shape
    n_i = s // ts
    hb = ts // HALO
    n_hb = s // HALO
    kern = functools.partial(_post_kernel, ts=ts)

    def three(c):
        return [
            pl.BlockSpec((1, HALO, c), lambda bi, i: (bi, jnp.maximum(i * hb - 1, 0), 0)),
            pl.BlockSpec((1, ts, c), lambda bi, i: (bi, i, 0)),
            pl.BlockSpec((1, HALO, c), lambda bi, i: (bi, jnp.minimum((i + 1) * hb, n_hb - 1), 0)),
        ]

    rows = ts + 2 * SUB
    assert rows % (2 * SUB) == 0 and (rows // SUB) % 8 != 0
    w_list = [wts["wo"], wts["g2"], wts["wup"], wts["conv_f"], wts["wdn"], wts["gf"]]
    return pl.pallas_call(
        kern,
        grid=(b, n_i),
        in_specs=three(D_MODEL) + three(A_W) + three(MLA_W)
        + [pl.BlockSpec((1, 6, D_MODEL), lambda bi, i: (bi, 0, 0))]
        + [_const_spec(w.shape) for w in w_list],
        out_specs=pl.BlockSpec((1, ts, D_MODEL), lambda bi, i: (bi, i, 0)),
        out_shape=jax.ShapeDtypeStruct((b, s, D_MODEL), F32),
        scratch_shapes=[pltpu.VMEM((ts + 2 * HALO, MIX_W), BF16),
                        pltpu.VMEM((D_MODEL // LANES, rows, LANES), F32),
                        pltpu.VMEM((rows, D_MODEL), BF16),
                        pltpu.VMEM((2, rows, 2 * FF_CHUNK), F32),
                        pltpu.VMEM((rows, D_MODEL), F32)],
        compiler_params=pltpu.CompilerParams(dimension_semantics=("parallel", "arbitrary"),
                                             vmem_limit_bytes=VMEM_LIMIT),
        name="post_mlp",
    )(x, x, x, ya, ya, ya, yb, yb, yb, mod, *w_list)


def _rot_half(w, axis):
    x1, x2 = jnp.split(w, 2, axis=axis)
    return jnp.concatenate([-x2, x1], axis=axis)


def _prep_weights(norm1_g, w_in, conv_a_w, q_norm_g, w_uq, kv_norm_g, w_ukv, out_norm_a_g, out_norm_b_g,
                  w_o, norm2_g, w_up, ffn_conv_w, w_down, final_g):
    w_in, w_uq, w_ukv, w_o, w_up, w_down = (w.astype(BF16) for w in (w_in, w_uq, w_ukv, w_o, w_up, w_down))
    cuts = np.cumsum([A_W, A_W, A_W, Q_LORA, KV_LORA]).tolist()
    w_h, w_b, w_c, w_cq, w_ckv, w_kr = jnp.split(w_in, cuts, axis=1)
    w1 = jnp.concatenate([w_h, w_c], axis=1)
    w2 = jnp.concatenate([w_b, w_cq, w_ckv, w_kr, _rot_half(w_kr, 1)], axis=1)

    wq3 = w_uq.reshape(Q_LORA, N_HEADS, QK_NOPE + QK_ROPE)
    wq3 = jnp.concatenate([wq3, _rot_half(wq3[:, :, QK_NOPE:], 2)], axis=2)
    wqt = wq3.reshape(Q_LORA, N_HEADS * HEAD_W).T

    wkv3 = w_ukv.reshape(KV_LORA, N_HEADS, QK_NOPE + V_DIM)
    wk = wkv3[:, :, :QK_NOPE].reshape(KV_LORA, N_HEADS * QK_NOPE)
    wvt = wkv3[:, :, QK_NOPE:].reshape(KV_LORA, N_HEADS * V_DIM).T

    wup = w_up.reshape(D_MODEL, 2, N_FF_CHUNKS, FF_CHUNK).transpose(2, 0, 1, 3).reshape(
        N_FF_CHUNKS, D_MODEL, 2 * FF_CHUNK)
    conv_f = ffn_conv_w.reshape(3, 2, N_FF_CHUNKS, FF_CHUNK).transpose(2, 0, 1, 3).reshape(
        N_FF_CHUNKS, 3, 2 * FF_CHUNK)
    wdn = w_down.reshape(N_FF_CHUNKS, FF_CHUNK, D_MODEL)
    row = lambda g: g.reshape(1, -1)
    return {
        "g1": row(norm1_g), "w1": w1, "w2": w2, "conv_a": conv_a_w, "gq": row(q_norm_g), "wqt": wqt,
        "gkv": row(kv_norm_g), "wk": wk, "wvt": wvt, "ga": row(out_norm_a_g), "gb": row(out_norm_b_g), "wo": w_o,
        "g2": row(norm2_g), "wup": wup, "conv_f": conv_f, "wdn": wdn, "gf": row(final_g),
    }


def _rope_table(s):
    inv = 1.0 / (ROPE_THETA ** (jnp.arange(0, QK_ROPE, 2, dtype=F32) / QK_ROPE))
    ang = jnp.arange(s, dtype=F32)[:, None] * inv[None, :]
    cos, sin = jnp.cos(ang), jnp.sin(ang)
    return jnp.concatenate([cos, cos, sin, sin], axis=1)


def _tiles(s):
    return min(512, s), min(512, s)


def _forward(x, mod, wts):
    s = x.shape[1]
    ts, tq = _tiles(s)
    ya, qt, k, kr, vt = _pre_call(x, mod, _rope_table(s), wts, ts=ts)
    yb = _attn_call(qt, k, kr, vt, wts["gb"], tq=tq)
    return _post_call(x, ya, yb, mod, wts, ts=ts)


def kernel(x_prompt, x_sample, c_prompt, c_sample, w_ada, b_ada, norm1_g, w_in, conv_a_w, q_norm_g, w_uq,
           kv_norm_g, w_ukv, out_norm_a_g, out_norm_b_g, w_o, norm2_g, w_up, ffn_conv_w, w_down, final_g):
    assert w_ada.shape[0] == 1, "single-layer kernel"
    wts = _prep_weights(norm1_g[0], w_in[0], conv_a_w[0], q_norm_g[0], w_uq[0], kv_norm_g[0], w_ukv[0],
                        out_norm_a_g[0], out_norm_b_g[0], w_o[0], norm2_g[0], w_up[0], ffn_conv_w[0],
                        w_down[0], final_g)
    nbp = c_prompt.shape[0]
    c_all = jnp.concatenate([c_prompt, c_sample], axis=0)
    mod = _mod_call(c_all, w_ada[0].astype(BF16), b_ada[0].reshape(1, -1))
    mod = mod.reshape(c_all.shape[0], 6, D_MODEL)
    y_prompt = _forward(x_prompt, mod[:nbp], wts)
    y_sample = _forward(x_sample, mod[nbp:], wts)
    return (y_prompt, y_sample)
```
